```python
import math
import jax, jax.numpy as jnp
from jax import lax
import numpy as np

D_MODEL = 1024
BATCH = 2
SEQ = 8192
DEPTH = 2

D_MIX = D_MODEL
D_HYENA = D_MIX // 2
D_ATTN = D_MIX - D_HYENA
N_DIFF_HEADS = 4
DIFF_V_DIM = D_ATTN // N_DIFF_HEADS
DIFF_QK_DIM = DIFF_V_DIM // 2
D_QK = N_DIFF_HEADS * 2 * DIFF_QK_DIM
HYENA_ORDER = 2
HY_PROJ = (HYENA_ORDER + 1) * D_HYENA
D_IN = HY_PROJ + D_HYENA + 2 * D_QK + D_ATTN + D_ATTN
HYENA_EMB_DIM = 33
HYENA_FILTER_ORDER = 64
SHORT_CONV = 3
DECAY_TARGET = 1e-2
FAST_DECAY_PCT = 0.3
SLOW_DECAY_PCT = 1.5
ROPE_THETA = 10000.0
Q_BLOCK = 128
NORM_EPS = 1e-6
SUBLN_EPS = 1e-5

kernel_name = "hybrid_hyena_diffattn_encoder"


def rmsnorm(x, g, eps):
    xf = x.astype(jnp.float32)
    y = xf * lax.rsqrt(jnp.mean(xf * xf, axis=-1, keepdims=True) + eps)
    return (y * g.astype(jnp.float32)).astype(x.dtype)


def rope_tables(S, dim):
    inv_freq = 1.0 / (ROPE_THETA ** (jnp.arange(0, dim, 2, dtype=jnp.float32) / dim))
    pos = jnp.arange(S, dtype=jnp.float32)
    ang = pos[:, None] * inv_freq[None, :]
    ang = jnp.concatenate([ang, ang], axis=-1)
    return jnp.cos(ang), jnp.sin(ang)


def apply_rope(t, cos, sin):
    half = t.shape[-1] // 2
    rot = jnp.concatenate([-t[..., half:], t[..., :half]], axis=-1)
    c = cos[None, :, None, None, :]
    s = sin[None, :, None, None, :]
    return (t * c + rot * s).astype(t.dtype)


def short_conv_centered(u, w, b):
    up = jnp.pad(u, ((0, 0), (1, 1), (0, 0)))
    return up[:, :-2] * w[0] + up[:, 1:-1] * w[1] + up[:, 2:] * w[2] + b


def hyena_filter_spectra(L, w1, b1, w2, b2, w3, b3, freq, w4):
    f32 = jnp.float32
    t = jnp.linspace(0.0, 1.0, L, dtype=f32)[:, None]
    bands = (HYENA_EMB_DIM - 1) // 2
    wpos = 2.0 * math.pi * jnp.arange(L, dtype=f32) / L
    fb = jnp.linspace(1e-4, bands - 1, bands, dtype=f32)
    ph = wpos[:, None] * fb[None, :]
    z = jnp.concatenate([t, jnp.cos(ph), -jnp.sin(ph)], axis=-1)
    fr = freq.astype(f32)
    hdn = jnp.sin(fr * (z @ w1.astype(f32) + b1.astype(f32)))
    hdn = jnp.sin(fr * (hdn @ w2.astype(f32) + b2.astype(f32)))
    hdn = jnp.sin(fr * (hdn @ w3.astype(f32) + b3.astype(f32)))
    hf = (hdn @ w4.astype(f32)).reshape(L, 2 * HYENA_ORDER, D_HYENA)
    max_decay = math.log(DECAY_TARGET) / FAST_DECAY_PCT
    min_decay = math.log(DECAY_TARGET) / SLOW_DECAY_PCT
    deltas = jnp.linspace(min_decay, max_decay, D_HYENA, dtype=f32)
    decay = jnp.exp(-t * jnp.abs(deltas)[None, :])
    hf = hf * decay[:, None, :]
    h_fwd = hf[:, :HYENA_ORDER]
    h_bwd = hf[:, HYENA_ORDER:]
    k = jnp.concatenate([h_fwd, jnp.zeros((1, HYENA_ORDER, D_HYENA), f32), h_bwd[:0:-1]], axis=0)
    k = k / jnp.sum(jnp.abs(k), axis=0, keepdims=True)
    return jnp.fft.rfft(k, axis=0)


def fft_conv(u, kf, dskip):
    L = u.shape[1]
    U = jnp.fft.rfft(u, n=2 * L, axis=1)
    y = jnp.fft.irfft(U * kf[None], n=2 * L, axis=1)[:, :L]
    return y + u * dskip


def hyena_branch(hy_in, conv_w, conv_b, kf, filt_bias):
    f32 = jnp.float32
    c = short_conv_centered(hy_in.astype(f32), conv_w.astype(f32), conv_b.astype(f32))
    pv, px1, px2 = jnp.split(c, 3, axis=-1)
    dsk = filt_bias.astype(f32)
    z = px1 * fft_conv(pv, kf[:, 0], dsk[0])
    z = px2 * fft_conv(z, kf[:, 1], dsk[1])
    return z.astype(hy_in.dtype)


def diff_attention(q, k, v, lam):
    B, S, H, _, dq = q.shape
    nb = S // Q_BLOCK
    scale = dq ** -0.5
    qb = q.reshape(B, nb, Q_BLOCK, H, 2, dq).transpose(1, 0, 2, 3, 4, 5)

    def block(qi):
        s = jnp.einsum('bqhcd,bkhcd->bhcqk', qi, k, preferred_element_type=jnp.float32) * scale
        p = jax.nn.softmax(s, axis=-1)
        a = p[:, :, 0] - lam * p[:, :, 1]
        return jnp.einsum('bhqk,bkhe->bqhe', a.astype(v.dtype), v)

    o = lax.map(block, qb)
    return o.transpose(1, 0, 2, 3, 4).reshape(B, S, H, v.shape[-1])


def diff_attn_branch(q, k, v, cos, sin, lq1, lk1, lq2, lk2, subg, lam_init):
    B, S, _ = q.shape
    qh = apply_rope(q.reshape(B, S, N_DIFF_HEADS, 2, DIFF_QK_DIM), cos, sin)
    kh = apply_rope(k.reshape(B, S, N_DIFF_HEADS, 2, DIFF_QK_DIM), cos, sin)
    vh = v.reshape(B, S, N_DIFF_HEADS, DIFF_V_DIM)
    f32 = jnp.float32
    lam = (jnp.exp(jnp.sum(lq1.astype(f32) * lk1.astype(f32)))
           - jnp.exp(jnp.sum(lq2.astype(f32) * lk2.astype(f32))) + lam_init)
    o = diff_attention(qh, kh, vh, lam)
    o = rmsnorm(o, subg, SUBLN_EPS) * (1.0 - lam_init)
    return o.reshape(B, S, D_ATTN).astype(q.dtype)


def setup_inputs(seed: int = 0) -> dict:
    key = jax.random.key(seed)
    ks = jax.random.split(key, 24)
    n = jax.random.normal
    f32 = jnp.float32
    E, R = HYENA_EMB_DIM, HYENA_FILTER_ORDER
    return {
        "x": n(ks[0], (BATCH, SEQ, D_MODEL), f32),
        "norm_g": 1.0 + 0.02 * n(ks[1], (DEPTH, D_MODEL), f32),
        "w_in": n(ks[2], (DEPTH, D_MODEL, D_IN), f32) * D_MODEL ** -0.5,
        "conv_w": n(ks[3], (DEPTH, SHORT_CONV, HY_PROJ), f32) * SHORT_CONV ** -0.5,
        "conv_b": 0.05 * n(ks[4], (DEPTH, HY_PROJ), f32),
        "filt_w1": n(ks[5], (DEPTH, E, R), f32) * E ** -0.5,
        "filt_b1": n(ks[6], (DEPTH, R), f32) * E ** -0.5,
        "filt_w2": n(ks[7], (DEPTH, R, R), f32) * R ** -0.5,
        "filt_b2": n(ks[8], (DEPTH, R), f32) * R ** -0.5,
        "filt_w3": n(ks[9], (DEPTH, R, R), f32) * R ** -0.5,
        "filt_b3": n(ks[10], (DEPTH, R), f32) * R ** -0.5,
        "filt_freq": 1.0 + 0.01 * n(ks[11], (DEPTH, R), f32),
        "filt_w4": n(ks[12], (DEPTH, R, 2 * HYENA_ORDER * D_HYENA), f32) * R ** -0.5,
        "filt_bias": n(ks[13], (DEPTH, HYENA_ORDER, D_HYENA), f32),
        "lam_q1": 0.1 * n(ks[14], (DEPTH, DIFF_QK_DIM), f32),
        "lam_k1": 0.1 * n(ks[15], (DEPTH, DIFF_QK_DIM), f32),
        "lam_q2": 0.1 * n(ks[16], (DEPTH, DIFF_QK_DIM), f32),
        "lam_k2": 0.1 * n(ks[17], (DEPTH, DIFF_QK_DIM), f32),
        "subln_g": 1.0 + 0.02 * n(ks[18], (DEPTH, DIFF_V_DIM), f32),
        "w_out": n(ks[19], (DEPTH, D_MIX, D_MODEL), f32) * D_MIX ** -0.5,
        "final_g": 1.0 + 0.02 * n(ks[20], (D_MODEL,), f32),
    }


def reference(x, norm_g, w_in, conv_w, conv_b, filt_w1, filt_b1, filt_w2, filt_b2, filt_w3, filt_b3,
              filt_freq, filt_w4, filt_bias, lam_q1, lam_k1, lam_q2, lam_k2, subln_g, w_out, final_g):
    B, S, _ = x.shape
    cos, sin = rope_tables(S, DIFF_QK_DIM)
    o_hg = HY_PROJ
    o_q = o_hg + D_HYENA
    o_k = o_q + D_QK
    o_v = o_k + D_QK
    o_ag = o_v + D_ATTN
    h = x
    for layer in range(DEPTH):
        lam_init = 0.8 - 0.6 * math.exp(-0.3 * layer)
        u = rmsnorm(h, norm_g[layer], NORM_EPS)
        proj = jnp.einsum('bsd,de->bse', u, w_in[layer])
        hy_in = proj[..., :o_hg]
        hy_gate = proj[..., o_hg:o_q]
        q = proj[..., o_q:o_k]
        k = proj[..., o_k:o_v]
        v = proj[..., o_v:o_ag]
        at_gate = proj[..., o_ag:]
        kf = hyena_filter_spectra(S, filt_w1[layer], filt_b1[layer], filt_w2[layer], filt_b2[layer],
                                  filt_w3[layer], filt_b3[layer], filt_freq[layer], filt_w4[layer])
        y_h = hyena_branch(hy_in, conv_w[layer], conv_b[layer], kf, filt_bias[layer]) * jax.nn.silu(hy_gate)
        y_a = diff_attn_branch(q, k, v, cos, sin, lam_q1[layer], lam_k1[layer], lam_q2[layer],
                               lam_k2[layer], subln_g[layer], lam_init) * jax.nn.silu(at_gate)
        y = jnp.concatenate([y_h, y_a], axis=-1)
        h = h + jnp.einsum('bsm,md->bsd', y, w_out[layer])
    return rmsnorm(h, final_g, NORM_EPS)
```

```python
import functools
import math

import numpy as np
import jax
import jax.numpy as jnp
from jax import lax
from jax.experimental import pallas as pl
from jax.experimental.pallas import tpu as pltpu

F32 = jnp.float32
BF16 = jnp.bfloat16

D_MODEL = 1024
D_HYENA = 512
D_ATTN = 512
N_HEADS = 4
V_DIM = 128
QK_DIM = 64
HY_PROJ = 3 * D_HYENA
D_IN = 4096
EMB_DIM = 33
FILT_ORDER = 64
DECAY_TARGET = 1e-2
FAST_DECAY_PCT = 0.3
SLOW_DECAY_PCT = 1.5
ROPE_THETA = 10000.0
NORM_EPS = 1e-6
SUBLN_EPS = 1e-5

LANES = 128
VMEM_LIMIT = 48 * 1024 * 1024


def _params(*sem):
    return pltpu.CompilerParams(dimension_semantics=sem, vmem_limit_bytes=VMEM_LIMIT)


def _in_proj_kernel(x_ref, g_ref, w_ref, o_ref, u_ref):
    @pl.when(pl.program_id(1) == 0)
    def _():
        x = x_ref[...]
        ms = jnp.mean(x * x, axis=-1, keepdims=True)
        u_ref[...] = (x * lax.rsqrt(ms + NORM_EPS) * g_ref[...]).astype(BF16)

    o_ref[...] = jnp.dot(u_ref[...], w_ref[...], preferred_element_type=F32)


def in_proj(x2, g, w_bf16, tm=512, tn=1024):
    m, d = x2.shape
    n = w_bf16.shape[1]
    return pl.pallas_call(
        _in_proj_kernel,
        grid=(m // tm, n // tn),
        in_specs=[pl.BlockSpec((tm, d), lambda i, j: (i, 0)),
                  pl.BlockSpec((1, d), lambda i, j: (0, 0)),
                  pl.BlockSpec((d, tn), lambda i, j: (0, j))],
        out_specs=pl.BlockSpec((tm, tn), lambda i, j: (i, j)),
        out_shape=jax.ShapeDtypeStruct((m, n), F32),
        scratch_shapes=[pltpu.VMEM((tm, d), BF16)],
        compiler_params=_params("parallel", "arbitrary"),
        name="in_proj",
    )(x2, g.reshape(1, d), w_bf16)


def _rope_kernel(q_ref, k_ref, v_ref, cos_ref, sin_ref, q0_ref, q1_ref, ko_ref, vo_ref):
    cos = cos_ref[...]
    sin = sin_ref[...]
    lane = lax.broadcasted_iota(jnp.int32, cos.shape, 1)
    first = (lane & (QK_DIM - 1)) < QK_DIM // 2
    map0 = lane < QK_DIM
    for h in range(N_HEADS):
        sl = slice(h * LANES, (h + 1) * LANES)
        for src, is_q in ((q_ref, True), (k_ref, False)):
            t = src[:, sl]
            rot = jnp.where(first, pltpu.roll(t, LANES - QK_DIM // 2, axis=1),
                            pltpu.roll(t, QK_DIM // 2, axis=1))
            r = t * cos + rot * sin
            if is_q:
                r = r * (QK_DIM ** -0.5)
                q0_ref[:, sl] = jnp.where(map0, r, 0.0).astype(BF16)
                q1_ref[:, sl] = jnp.where(map0, 0.0, r).astype(BF16)
            else:
                ko_ref[:, sl] = r.astype(BF16)
    vo_ref[...] = v_ref[...].astype(BF16)


def rope(proj, cos_t, sin_t, seq, tm=512):
    m = proj.shape[0]
    w = N_HEADS * LANES
    nq = D_IN // w
    spb = seq // tm
    out = jax.ShapeDtypeStruct((m, w), BF16)
    ospec = pl.BlockSpec((tm, w), lambda i: (i, 0))
    return pl.pallas_call(
        _rope_kernel,
        grid=(m // tm,),
        in_specs=[pl.BlockSpec((tm, w), lambda i: (i, nq - 4)),
                  pl.BlockSpec((tm, w), lambda i: (i, nq - 3)),
                  pl.BlockSpec((tm, w), lambda i: (i, nq - 2)),
                  pl.BlockSpec((tm, LANES), lambda i: (i % spb, 0)),
                  pl.BlockSpec((tm, LANES), lambda i: (i % spb, 0))],
        out_specs=[ospec, ospec, ospec, ospec],
        out_shape=[out, out, out, out],
        compiler_params=_params("parallel"),
        name="rope",
    )(proj, proj, proj, cos_t, sin_t)


def _attn_kernel(q0_ref, q1_ref, k_ref, v_ref, gate_ref, lq1_ref, lk1_ref, lq2_ref, lk2_ref,
                 subg_ref, o_ref, *, lam_init, tk):
    tq = q0_ref.shape[0]
    nk = k_ref.shape[0] // tk
    q0 = q0_ref[...]
    q1 = q1_ref[...]

    def step(q, kj, vj, m, l, acc):
        s = lax.dot_general(q, kj, (((1,), (1,)), ((), ())), preferred_element_type=F32)
        m_new = jnp.maximum(m, jnp.max(s, axis=-1, keepdims=True))
        alpha = jnp.exp(m - m_new)
        p = jnp.exp(s - m_new)
        l = alpha * l + jnp.sum(p, axis=-1, keepdims=True)
        acc = alpha * acc + jnp.dot(p.astype(BF16), vj, preferred_element_type=F32)
        return m_new, l, acc

    def body(j, carry):
        m0, l0, a0, m1, l1, a1 = carry
        start = pl.multiple_of(j * tk, tk)
        kj = k_ref[pl.ds(start, tk), :]
        vj = v_ref[pl.ds(start, tk), :]
        m0, l0, a0 = step(q0, kj, vj, m0, l0, a0)
        m1, l1, a1 = step(q1, kj, vj, m1, l1, a1)
        return m0, l0, a0, m1, l1, a1

    neg = jnp.full((tq, 1), -1e30, F32)
    zero1 = jnp.zeros((tq, 1), F32)
    zacc = jnp.zeros((tq, V_DIM), F32)
    _, l0, a0, _, l1, a1 = lax.fori_loop(0, nk, body, (neg, zero1, zacc, neg, zero1, zacc))

    lam = (jnp.exp(jnp.sum(lq1_ref[...] * lk1_ref[...], axis=-1, keepdims=True))
           - jnp.exp(jnp.sum(lq2_ref[...] * lk2_ref[...], axis=-1, keepdims=True)) + lam_init)
    o = a0 / l0 - lam * (a1 / l1)
    o = o * lax.rsqrt(jnp.mean(o * o, axis=-1, keepdims=True) + SUBLN_EPS)
    o = o * subg_ref[...] * (1.0 - lam_init)
    g = gate_ref[...]
    o_ref[...] = (o * (g / (1.0 + jnp.exp(-g)))).astype(o_ref.dtype)


def attention(q0, q1, k, v, proj3, lq1, lk1, lq2, lk2, subg, lam_init, tq=256, tk=512):
    b, s, _ = q0.shape
    gate_blk = (D_IN - D_ATTN) // LANES
    qspec = pl.BlockSpec((None, tq, LANES), lambda bi, h, qi: (bi, qi, h))
    kvspec = pl.BlockSpec((None, s, LANES), lambda bi, h, qi: (bi, 0, h))
    small = pl.BlockSpec((1, QK_DIM), lambda bi, h, qi: (0, 0))
    return pl.pallas_call(
        functools.partial(_attn_kernel, lam_init=lam_init, tk=tk),
        grid=(b, N_HEADS, s // tq),
        in_specs=[qspec, qspec, kvspec, kvspec,
                  pl.BlockSpec((None, tq, LANES), lambda bi, h, qi: (bi, qi, gate_blk + h)),
                  small, small, small, small,
                  pl.BlockSpec((1, V_DIM), lambda bi, h, qi: (0, 0))],
        out_specs=pl.BlockSpec((None, tq, LANES), lambda bi, h, qi: (bi, qi, h)),
        out_shape=jax.ShapeDtypeStruct((b, s, D_ATTN), BF16),
        compiler_params=_params("parallel", "parallel", "arbitrary"),
        name="diff_attn",
    )(q0, q1, k, v, proj3, lq1.reshape(1, -1), lk1.reshape(1, -1), lq2.reshape(1, -1),
      lk2.reshape(1, -1), subg.reshape(1, -1))


def _short_conv_kernel(u_ref, w_ref, b_ref, o_ref):
    u = u_ref[...]
    n = u.shape[0]
    row = lax.broadcasted_iota(jnp.int32, u.shape, 0)
    prev = jnp.where(row == 0, 0.0, pltpu.roll(u, 1, axis=0))
    nxt = jnp.where(row == n - 1, 0.0, pltpu.roll(u, n - 1, axis=0))
    o_ref[...] = prev * w_ref[0:1, :] + u * w_ref[1:2, :] + nxt * w_ref[2:3, :] + b_ref[...]


def short_conv(proj3, conv_w, conv_b):
    b, s, _ = proj3.shape
    nblk = HY_PROJ // LANES
    return pl.pallas_call(
        _short_conv_kernel,
        grid=(b, nblk),
        in_specs=[pl.BlockSpec((None, s, LANES), lambda bi, j: (bi, 0, j)),
                  pl.BlockSpec((3, LANES), lambda bi, j: (0, j)),
                  pl.BlockSpec((1, LANES), lambda bi, j: (0, j))],
        out_specs=pl.BlockSpec((None, s, LANES), lambda bi, j: (bi, 0, j)),
        out_shape=jax.ShapeDtypeStruct((b, s, HY_PROJ), F32),
        compiler_params=_params("parallel", "parallel"),
        name="short_conv",
    )(proj3, conv_w, conv_b.reshape(1, -1))


def _filter_kernel(z_ref, w1_ref, b1_ref, w2_ref, b2_ref, w3_ref, b3_ref, fr_ref, w4_ref, nd_ref,
                   h_ref, asum_ref):
    i = pl.program_id(0)
    hi = lax.Precision.HIGHEST
    z = z_ref[...]
    fr = fr_ref[...]
    h = jnp.sin(fr * (jnp.dot(z, w1_ref[...], precision=hi, preferred_element_type=F32) + b1_ref[...]))
    h = jnp.sin(fr * (jnp.dot(h, w2_ref[...], precision=hi, preferred_element_type=F32) + b2_ref[...]))
    h = jnp.sin(fr * (jnp.dot(h, w3_ref[...], precision=hi, preferred_element_type=F32) + b3_ref[...]))
    hf = jnp.dot(h, w4_ref[...], precision=hi, preferred_element_type=F32)
    decay = jnp.exp(z[:, 0:1] * nd_ref[...])
    tt = z.shape[0]
    row = lax.broadcasted_iota(jnp.int32, (tt, D_HYENA), 0) + i * tt

    @pl.when(i == 0)
    def _():
        asum_ref[...] = jnp.zeros_like(asum_ref)

    for g in range(4):
        sl = slice(g * D_HYENA, (g + 1) * D_HYENA)
        hg = hf[:, sl] * decay
        if g >= 2:
            hg = jnp.where(row == 0, 0.0, hg)
        h_ref[:, sl] = hg
        asum_ref[:, sl] += jnp.sum(jnp.abs(hg), axis=0, keepdims=True)


def filter_taps(z_pad, w1p, b1, w2, b2, w3, b3, fr, w4, neg_abs_delta, tt=512):
    seq = z_pad.shape[0]
    r = FILT_ORDER
    full = lambda shp: pl.BlockSpec(shp, lambda i: (0, 0))
    return pl.pallas_call(
        _filter_kernel,
        grid=(seq // tt,),
        in_specs=[pl.BlockSpec((tt, LANES), lambda i: (i, 0)),
                  full((LANES, r)), full((1, r)), full((r, r)), full((1, r)), full((r, r)), full((1, r)),
                  full((1, r)), full((r, 4 * D_HYENA)), full((1, D_HYENA))],
        out_specs=[pl.BlockSpec((tt, 4 * D_HYENA), lambda i: (i, 0)),
                   pl.BlockSpec((1, 4 * D_HYENA), lambda i: (0, 0))],
        out_shape=[jax.ShapeDtypeStruct((seq, 4 * D_HYENA), F32),
                   jax.ShapeDtypeStruct((1, 4 * D_HYENA), F32)],
        compiler_params=_params("arbitrary"),
        name="hyena_filter",
    )(z_pad, w1p, b1.reshape(1, r), w2, b2.reshape(1, r), w3, b3.reshape(1, r), fr.reshape(1, r), w4,
      neg_abs_delta)


def _stack_c(m):
    return np.block([[m.real, -m.imag], [m.imag, m.real]])


@functools.lru_cache(maxsize=None)
def _dft_tables(seq):
    n = 2 * seq
    r1 = LANES
    r2h = seq // r1
    n1 = np.arange(r1)
    k2 = np.arange(r1)
    n2 = np.arange(r2h)
    e = (k2[None, :, None] * (n1[:, None, None] + r1 * n2[None, None, :])) % n
    t = np.exp(-2j * np.pi * e / n)
    g_fwd = np.concatenate([np.concatenate([t.real, -t.imag], axis=2),
                            np.concatenate([t.imag, t.real], axis=2)], axis=1)
    zero = np.zeros_like(t.real)
    g_filt = np.concatenate([np.concatenate([t.real, zero], axis=2),
                             np.concatenate([t.imag, zero], axis=2),
                             np.concatenate([zero, t.real], axis=2),
                             np.concatenate([zero, -t.imag], axis=2)], axis=1)
    f = np.exp(-2j * np.pi * ((n1[:, None] * n1[None, :]) % r1) / r1)
    m3 = _stack_c(f)
    m3_filt = np.block([[f.real, -f.imag, f.real, f.imag],
                        [f.imag, f.real, -f.imag, f.real]])
    e = (n1[None, :, None] * (r1 * n1[None, None, :] + k2[:, None, None])) % n
    einv = np.exp(2j * np.pi * e / n)
    g_inv = np.concatenate([np.concatenate([einv.real, -einv.imag], axis=2),
                            np.concatenate([einv.imag, einv.real], axis=2)], axis=1)
    d = np.exp(2j * np.pi * ((n2[:, None] * k2[None, :]) % r1) / r1)
    m5 = _stack_c(d)
    tabs = dict(g_fwd=g_fwd, g_filt=g_filt, m3=m3, m3_filt=m3_filt, g_inv=g_inv, m5=m5)
    return {k: v.astype(np.float32) for k, v in tabs.items()}


GRP = 8


def _filt_stage1_kernel(xf_ref, xb_ref, sf_ref, sb_ref, g_ref, o_ref):
    scale = 1.0 / (sf_ref[...] + sb_ref[...])
    r1 = o_ref.shape[2]
    for j in range(GRP):
        zin = jnp.concatenate([xf_ref[:, j, :], xb_ref[:, j, :]], axis=0) * scale
        y = jnp.dot(g_ref[j], zin.astype(BF16), preferred_element_type=F32)
        for p in range(4):
            o_ref[p, j] = y[p * r1:(p + 1) * r1]


def _filt_stage2_kernel(b_ref, m_ref, o_ref, *, inv_n):
    r1 = o_ref.shape[2]
    for j in range(GRP):
        zin = jnp.concatenate([b_ref[p, :, j, :] for p in range(4)], axis=0).astype(BF16)
        x = jnp.dot(m_ref[...], zin, preferred_element_type=F32) * inv_n
        o_ref[0, j] = x[:r1]
        o_ref[1, j] = x[r1:]


def filter_spectrum(taps, asum, tabs, seq):
    c = D_HYENA
    r1 = LANES
    r2h = seq // r1
    x4 = taps.reshape(r2h, r1, 4 * c)
    s1 = pl.pallas_call(
        _filt_stage1_kernel,
        grid=(2, r1 // GRP),
        in_specs=[pl.BlockSpec((r2h, GRP, c), lambda o, i: (0, i, o)),
                  pl.BlockSpec((r2h, GRP, c), lambda o, i: (0, i, 2 + o)),
                  pl.BlockSpec((1, c), lambda o, i: (0, o)),
                  pl.BlockSpec((1, c), lambda o, i: (0, 2 + o)),
                  pl.BlockSpec((GRP, 4 * r1, 2 * r2h), lambda o, i: (i, 0, 0))],
        out_specs=pl.BlockSpec((None, 4, GRP, r1, c), lambda o, i: (o, 0, i, 0, 0)),
        out_shape=jax.ShapeDtypeStruct((2, 4, r1, r1, c), F32),
        compiler_params=_params("parallel", "parallel"),
        name="filt_dft1",
    )(x4, x4, asum, asum, tabs["g_filt"])
    return pl.pallas_call(
        functools.partial(_filt_stage2_kernel, inv_n=1.0 / (2 * seq)),
        grid=(2, r1 // GRP),
        in_specs=[pl.BlockSpec((None, 4, r1, GRP, c), lambda o, i: (o, 0, 0, i, 0)),
                  pl.BlockSpec((2 * r1, 4 * r1), lambda o, i: (0, 0))],
        out_specs=pl.BlockSpec((None, 2, GRP, r1, c), lambda o, i: (o, 0, i, 0, 0)),
        out_shape=jax.ShapeDtypeStruct((2, 2, r1, r1, c), F32),
        compiler_params=_params("parallel", "parallel"),
        name="filt_dft2",
    )(s1, tabs["m3_filt"])


def _conv_stage1_kernel(x_ref, g_ref, o_ref):
    r1 = o_ref.shape[2]
    for j in range(GRP):
        zin = jnp.concatenate([x_ref[0, :, j, :], x_ref[1, :, j, :]], axis=0).astype(BF16)
        y = jnp.dot(g_ref[j], zin, preferred_element_type=F32)
        o_ref[0, j] = y[:r1]
        o_ref[1, j] = y[r1:]


def _conv_stage2_kernel(b_ref, kf_ref, m3_ref, ginv_ref, o_ref):
    r1 = o_ref.shape[2]
    for j in range(GRP):
        zin = jnp.concatenate([b_ref[0, :, j, :], b_ref[1, :, j, :]], axis=0).astype(BF16)
        x = jnp.dot(m3_ref[...], zin, preferred_element_type=F32)
        xr, xi = x[:r1], x[r1:]
        kr, ki = kf_ref[0, j], kf_ref[1, j]
        y = jnp.concatenate([xr * kr - xi * ki, xr * ki + xi * kr], axis=0).astype(BF16)
        q = jnp.dot(ginv_ref[j], y, preferred_element_type=F32)
        o_ref[0, j] = q[:r1]
        o_ref[1, j] = q[r1:]


def _conv_stage3_mid_kernel(q_ref, m5_ref, u_ref, x1_ref, d_ref, o_ref):
    half = u_ref.shape[1]
    d = d_ref[...]
    for j in range(GRP):
        zin = jnp.concatenate([q_ref[0, :, j, :], q_ref[1, :, j, :]], axis=0).astype(BF16)
        y = jnp.dot(m5_ref[...], zin, preferred_element_type=F32)
        for bi in range(2):
            conv = y[bi * half:(bi + 1) * half] + u_ref[bi, :, j, :] * d
            o_ref[bi, :, j, :] = x1_ref[bi, :, j, :] * conv


def _conv_stage3_last_kernel(q_ref, m5_ref, u_ref, x2_ref, gate_ref, d_ref, o_ref):
    half = u_ref.shape[1]
    d = d_ref[...]
    for j in range(GRP):
        zin = jnp.concatenate([q_ref[0, :, j, :], q_ref[1, :, j, :]], axis=0).astype(BF16)
        y = jnp.dot(m5_ref[...], zin, preferred_element_type=F32)
        for bi in range(2):
            g = gate_ref[bi, :, j, :]
            conv = y[bi * half:(bi + 1) * half] + u_ref[bi, :, j, :] * d
            o_ref[bi, :, j, :] = x2_ref[bi, :, j, :] * conv * (g / (1.0 + jnp.exp(-g)))


def _conv_fwd(src4, blk, kf, tabs):
    c = D_HYENA
    r1 = LANES
    b, r2h = src4.shape[0], src4.shape[1]
    steps = r1 // GRP
    s1 = pl.pallas_call(
        _conv_stage1_kernel,
        grid=(steps,),
        in_specs=[pl.BlockSpec((b, r2h, GRP, c), lambda i: (0, 0, i, blk)),
                  pl.BlockSpec((GRP, 2 * r1, 2 * r2h), lambda i: (i, 0, 0))],
        out_specs=pl.BlockSpec((2, GRP, r1, c), lambda i: (0, i, 0, 0)),
        out_shape=jax.ShapeDtypeStruct((2, r1, r1, c), F32),
        compiler_params=_params("parallel"),
        name="conv_dft1",
    )(src4, tabs["g_fwd"])
    return pl.pallas_call(
        _conv_stage2_kernel,
        grid=(steps,),
        in_specs=[pl.BlockSpec((2, r1, GRP, c), lambda i: (0, 0, i, 0)),
                  pl.BlockSpec((2, GRP, r1, c), lambda i: (0, i, 0, 0)),
                  pl.BlockSpec((2 * r1, 2 * r1), lambda i: (0, 0)),
                  pl.BlockSpec((GRP, 2 * r1, 2 * r1), lambda i: (i, 0, 0))],
        out_specs=pl.BlockSpec((2, GRP, r1, c), lambda i: (0, i, 0, 0)),
        out_shape=jax.ShapeDtypeStruct((2, r1, r1, c), F32),
        compiler_params=_params("parallel"),
        name="conv_dft2",
    )(s1, kf, tabs["m3"], tabs["g_inv"])


def hyena_branch(csc, proj3, spec, filt_bias, tabs, seq):
    c = D_HYENA
    r1 = LANES
    r2h = seq // r1
    b = csc.shape[0]
    steps = r1 // GRP
    csc4 = csc.reshape(b, r2h, r1, 3 * c)
    proj4 = proj3.reshape(b, r2h, r1, D_IN)
    qspec = pl.BlockSpec((2, r1, GRP, c), lambda i: (0, 0, i, 0))
    m5spec = pl.BlockSpec((2 * r2h, 2 * r1), lambda i: (0, 0))
    dspec = pl.BlockSpec((1, c), lambda i: (0, 0))
    tblk = lambda blk: pl.BlockSpec((b, r2h, GRP, c), lambda i: (0, 0, i, blk))

    q1 = _conv_fwd(csc4, 0, spec[0], tabs)
    z = pl.pallas_call(
        _conv_stage3_mid_kernel,
        grid=(steps,),
        in_specs=[qspec, m5spec, tblk(0), tblk(1), dspec],
        out_specs=tblk(0),
        out_shape=jax.ShapeDtypeStruct((b, r2h, r1, c), F32),
        compiler_params=_params("parallel"),
        name="conv_idft_mid",
    )(q1, tabs["m5"], csc4, csc4, filt_bias[0:1])

    q2 = _conv_fwd(z, 0, spec[1], tabs)
    y = pl.pallas_call(
        _conv_stage3_last_kernel,
        grid=(steps,),
        in_specs=[qspec, m5spec, tblk(0), tblk(2), tblk(HY_PROJ // c), dspec],
        out_specs=tblk(0),
        out_shape=jax.ShapeDtypeStruct((b, r2h, r1, c), F32),
        compiler_params=_params("parallel"),
        name="conv_idft_last",
    )(q2, tabs["m5"], z, csc4, proj4, filt_bias[1:2])
    return y.reshape(b, seq, c)


def _out_proj_kernel(yh_ref, ya_ref, w_ref, h_ref, g_ref, o_ref, *, final):
    k = yh_ref.shape[1]
    acc = jnp.dot(yh_ref[...].astype(BF16), w_ref[:k, :], preferred_element_type=F32)
    acc = acc + jnp.dot(ya_ref[...].astype(BF16), w_ref[k:, :], preferred_element_type=F32)
    hn = h_ref[...] + acc
    if final:
        ms = jnp.mean(hn * hn, axis=-1, keepdims=True)
        hn = hn * lax.rsqrt(ms + NORM_EPS) * g_ref[...]
    o_ref[...] = hn


def out_proj(yh, ya, w_bf16, h2, g, final, tm=512):
    m, d = h2.shape
    k = yh.shape[1]
    return pl.pallas_call(
        functools.partial(_out_proj_kernel, final=final),
        grid=(m // tm,),
        in_specs=[pl.BlockSpec((tm, k), lambda i: (i, 0)),
                  pl.BlockSpec((tm, k), lambda i: (i, 0)),
                  pl.BlockSpec((2 * k, d), lambda i: (0, 0)),
                  pl.BlockSpec((tm, d), lambda i: (i, 0)),
                  pl.BlockSpec((1, d), lambda i: (0, 0))],
        out_specs=pl.BlockSpec((tm, d), lambda i: (i, 0)),
        out_shape=jax.ShapeDtypeStruct((m, d), F32),
        compiler_params=_params("parallel"),
        name="out_proj",
    )(yh, ya, w_bf16, h2, g.reshape(1, d))


def _rope_tables(seq):
    inv_freq = 1.0 / (ROPE_THETA ** (jnp.arange(0, QK_DIM, 2, dtype=F32) / QK_DIM))
    ang = jnp.arange(seq, dtype=F32)[:, None] * inv_freq[None, :]
    ang = jnp.concatenate([ang, ang, ang, ang], axis=-1)
    sign = jnp.where((jnp.arange(LANES) % QK_DIM) < QK_DIM // 2, -1.0, 1.0).astype(F32)
    return jnp.cos(ang), jnp.sin(ang) * sign[None, :]


def _filter_features(seq):
    t = jnp.linspace(0.0, 1.0, seq, dtype=F32)[:, None]
    bands = (EMB_DIM - 1) // 2
    wpos = 2.0 * math.pi * jnp.arange(seq, dtype=F32) / seq
    fb = jnp.linspace(1e-4, bands - 1, bands, dtype=F32)
    ph = wpos[:, None] * fb[None, :]
    z = jnp.concatenate([t, jnp.cos(ph), -jnp.sin(ph)], axis=-1)
    z_pad = jnp.pad(z, ((0, 0), (0, LANES - EMB_DIM)))
    max_decay = math.log(DECAY_TARGET) / FAST_DECAY_PCT
    min_decay = math.log(DECAY_TARGET) / SLOW_DECAY_PCT
    deltas = jnp.linspace(min_decay, max_decay, D_HYENA, dtype=F32)
    return z_pad, (-jnp.abs(deltas))[None, :]


def kernel(x, norm_g, w_in, conv_w, conv_b, filt_w1, filt_b1, filt_w2, filt_b2, filt_w3, filt_b3,
           filt_freq, filt_w4, filt_bias, lam_q1, lam_k1, lam_q2, lam_k2, subln_g, w_out, final_g):
    b, seq, d = x.shape
    depth = w_in.shape[0]
    m = b * seq
    tabs = {k: jnp.asarray(v).astype(BF16) for k, v in _dft_tables(seq).items()}
    cos_t, sin_t = _rope_tables(seq)
    z_pad, neg_abs_delta = _filter_features(seq)

    h = x.reshape(m, d)
    for layer in range(depth):
        lam_init = 0.8 - 0.6 * math.exp(-0.3 * layer)
        proj = in_proj(h, norm_g[layer], w_in[layer].astype(BF16))
        proj3 = proj.reshape(b, seq, D_IN)

        q0, q1, kr, vr = rope(proj, cos_t, sin_t, seq)
        sh = (b, seq, D_ATTN)
        y_a = attention(q0.reshape(sh), q1.reshape(sh), kr.reshape(sh), vr.reshape(sh), proj3,
                        lam_q1[layer], lam_k1[layer], lam_q2[layer], lam_k2[layer], subln_g[layer], lam_init)

        w1p = jnp.pad(filt_w1[layer], ((0, LANES - EMB_DIM), (0, 0)))
        taps, asum = filter_taps(z_pad, w1p, filt_b1[layer], filt_w2[layer], filt_b2[layer], filt_w3[layer],
                                 filt_b3[layer], filt_freq[layer], filt_w4[layer], neg_abs_delta)
        spec = filter_spectrum(taps, asum, tabs, seq)
        csc = short_conv(proj3, conv_w[layer], conv_b[layer])
        y_h = hyena_branch(csc, proj3, spec, filt_bias[layer], tabs, seq)

        h = out_proj(y_h.reshape(m, D_HYENA), y_a.reshape(m, D_ATTN), w_out[layer].astype(BF16), h,
                     final_g, final=(layer == depth - 1))
    return h.reshape(b, seq, d)
```

```python
import functools
import math

import numpy as np
import jax
import jax.numpy as jnp
from jax import lax
from jax.experimental import pallas as pl
from jax.experimental.pallas import tpu as pltpu

F32 = jnp.float32
BF16 = jnp.bfloat16

D_MODEL = 1024
D_HYENA = 512
D_ATTN = 512
N_HEADS = 4
V_DIM = 128
QK_DIM = 64
HY_PROJ = 3 * D_HYENA
D_IN = 4096
EMB_DIM = 33
FILT_ORDER = 64
DECAY_TARGET = 1e-2
FAST_DECAY_PCT = 0.3
SLOW_DECAY_PCT = 1.5
ROPE_THETA = 10000.0
NORM_EPS = 1e-6
SUBLN_EPS = 1e-5

LANES = 128
VMEM_LIMIT = 48 * 1024 * 1024


def _params(*sem):
    return pltpu.CompilerParams(dimension_semantics=sem, vmem_limit_bytes=VMEM_LIMIT)


def _in_proj_kernel(x_ref, g_ref, w_ref, o_ref, u_ref):
    @pl.when(pl.program_id(1) == 0)
    def _():
        x = x_ref[...]
        ms = jnp.mean(x * x, axis=-1, keepdims=True)
        u_ref[...] = (x * lax.rsqrt(ms + NORM_EPS) * g_ref[...]).astype(BF16)

    o_ref[...] = jnp.dot(u_ref[...], w_ref[...], preferred_element_type=F32)


def in_proj(x2, g, w_bf16, tm=1024, tn=1024):
    m, d = x2.shape
    n = w_bf16.shape[1]
    return pl.pallas_call(
        _in_proj_kernel,
        grid=(m // tm, n // tn),
        in_specs=[pl.BlockSpec((tm, d), lambda i, j: (i, 0)),
                  pl.BlockSpec((1, d), lambda i, j: (0, 0)),
                  pl.BlockSpec((d, tn), lambda i, j: (0, j))],
        out_specs=pl.BlockSpec((tm, tn), lambda i, j: (i, j)),
        out_shape=jax.ShapeDtypeStruct((m, n), F32),
        scratch_shapes=[pltpu.VMEM((tm, d), BF16)],
        compiler_params=_params("parallel", "arbitrary"),
        name="in_proj",
    )(x2, g.reshape(1, d), w_bf16)


LOG2E = 1.4426950408889634


def _rope_kernel(q_ref, k_ref, v_ref, cos_ref, sin_ref, q0_ref, q1_ref, ko_ref, vt_ref):
    cos = cos_ref[...]
    sin = sin_ref[...]
    lane = lax.broadcasted_iota(jnp.int32, cos.shape, 1)
    first = (lane & (QK_DIM - 1)) < QK_DIM // 2
    map0 = lane < QK_DIM
    for h in range(N_HEADS):
        sl = slice(h * LANES, (h + 1) * LANES)
        for src, is_q in ((q_ref, True), (k_ref, False)):
            t = src[:, sl]
            rot = jnp.where(first, pltpu.roll(t, LANES - QK_DIM // 2, axis=1),
                            pltpu.roll(t, QK_DIM // 2, axis=1))
            r = t * cos + rot * sin
            if is_q:
                r = r * (QK_DIM ** -0.5 * LOG2E)
                q0_ref[:, sl] = jnp.where(map0, r, 0.0).astype(BF16)
                q1_ref[:, sl] = jnp.where(map0, 0.0, r).astype(BF16)
            else:
                ko_ref[:, sl] = r.astype(BF16)
        vt_ref[h] = v_ref[:, sl].T.astype(BF16)


def rope(proj, cos_t, sin_t, batch, seq, tm=512):
    m = proj.shape[0]
    w = N_HEADS * LANES
    nq = D_IN // w
    spb = seq // tm
    out = jax.ShapeDtypeStruct((m, w), BF16)
    ospec = pl.BlockSpec((tm, w), lambda i: (i, 0))
    return pl.pallas_call(
        _rope_kernel,
        grid=(m // tm,),
        in_specs=[pl.BlockSpec((tm, w), lambda i: (i, nq - 4)),
                  pl.BlockSpec((tm, w), lambda i: (i, nq - 3)),
                  pl.BlockSpec((tm, w), lambda i: (i, nq - 2)),
                  pl.BlockSpec((tm, LANES), lambda i: (i % spb, 0)),
                  pl.BlockSpec((tm, LANES), lambda i: (i % spb, 0))],
        out_specs=[ospec, ospec, ospec,
                   pl.BlockSpec((None, N_HEADS, None, V_DIM, tm), lambda i: (i // spb, 0, i % spb, 0, 0))],
        out_shape=[out, out, out, jax.ShapeDtypeStruct((batch, N_HEADS, spb, V_DIM, tm), BF16)],
        compiler_params=_params("parallel"),
        name="rope",
    )(proj, proj, proj, cos_t, sin_t)


def _attn_kernel(q0_ref, q1_ref, k_ref, vt_ref, gate_ref, lq1_ref, lk1_ref, lq2_ref, lk2_ref,
                 subg_ref, o_ref, *, lam_init):
    tq = q0_ref.shape[0]
    nk, _, tk = vt_ref.shape
    q0 = q0_ref[...]
    q1 = q1_ref[...]

    def scores(q, kj):
        return lax.dot_general(kj, q, (((1,), (1,)), ((), ())), preferred_element_type=F32)

    def soft(st, m, l):
        m_new = jnp.maximum(m, jnp.max(st, axis=0, keepdims=True))
        alpha = jnp.exp2(m - m_new)
        p = jnp.exp2(st - m_new)
        l = alpha * l + jnp.sum(p, axis=0, keepdims=True)
        return m_new, l, alpha, p.astype(BF16)

    neg = jnp.full((1, tq), -1e30, F32)
    zero1 = jnp.zeros((1, tq), F32)
    zacc = jnp.zeros((V_DIM, tq), F32)
    m0, l0, a0, m1, l1, a1 = neg, zero1, zacc, neg, zero1, zacc
    k0 = k_ref[0:tk, :]
    nxt = (scores(q0, k0), scores(q1, k0))
    for j in range(nk):
        st0, st1 = nxt
        if j + 1 < nk:
            kn = k_ref[(j + 1) * tk:(j + 2) * tk, :]
            nxt = (scores(q0, kn), scores(q1, kn))
        vtj = vt_ref[j]
        m0, l0, al0, p0 = soft(st0, m0, l0)
        m1, l1, al1, p1 = soft(st1, m1, l1)
        a0 = al0 * a0 + jnp.dot(vtj, p0, preferred_element_type=F32)
        a1 = al1 * a1 + jnp.dot(vtj, p1, preferred_element_type=F32)

    lam = (jnp.exp(jnp.sum(lq1_ref[...] * lk1_ref[...], axis=-1, keepdims=True))
           - jnp.exp(jnp.sum(lq2_ref[...] * lk2_ref[...], axis=-1, keepdims=True)) + lam_init)
    ot = a0 / l0 - lam * (a1 / l1)
    ot = ot * lax.rsqrt(jnp.mean(ot * ot, axis=0, keepdims=True) + SUBLN_EPS)
    o = ot.T * subg_ref[...] * (1.0 - lam_init)
    g = gate_ref[...]
    o_ref[...] = (o * (g / (1.0 + jnp.exp(-g)))).astype(o_ref.dtype)


def attention(q0, q1, k, vt, proj3, lq1, lk1, lq2, lk2, subg, lam_init, tq=512):
    b, s, _ = q0.shape
    nk, tk = vt.shape[2], vt.shape[4]
    gate_blk = (D_IN - D_ATTN) // LANES
    qspec = pl.BlockSpec((None, tq, LANES), lambda bi, h, qi: (bi, qi, h))
    small = pl.BlockSpec((1, QK_DIM), lambda bi, h, qi: (0, 0))
    return pl.pallas_call(
        functools.partial(_attn_kernel, lam_init=lam_init),
        grid=(b, N_HEADS, s // tq),
        in_specs=[qspec, qspec,
                  pl.BlockSpec((None, s, LANES), lambda bi, h, qi: (bi, 0, h)),
                  pl.BlockSpec((None, None, nk, V_DIM, tk), lambda bi, h, qi: (bi, h, 0, 0, 0)),
                  pl.BlockSpec((None, tq, LANES), lambda bi, h, qi: (bi, qi, gate_blk + h)),
                  small, small, small, small,
                  pl.BlockSpec((1, V_DIM), lambda bi, h, qi: (0, 0))],
        out_specs=pl.BlockSpec((None, tq, LANES), lambda bi, h, qi: (bi, qi, h)),
        out_shape=jax.ShapeDtypeStruct((b, s, D_ATTN), BF16),
        compiler_params=_params("parallel", "parallel", "arbitrary"),
        name="diff_attn",
    )(q0, q1, k, vt, proj3, lq1.reshape(1, -1), lk1.reshape(1, -1), lq2.reshape(1, -1),
      lk2.reshape(1, -1), subg.reshape(1, -1))


def _short_conv_kernel(u_ref, w_ref, b_ref, o_ref):
    u = u_ref[...]
    n = u.shape[0]
    row = lax.broadcasted_iota(jnp.int32, u.shape, 0)
    prev = jnp.where(row == 0, 0.0, pltpu.roll(u, 1, axis=0))
    nxt = jnp.where(row == n - 1, 0.0, pltpu.roll(u, n - 1, axis=0))
    o_ref[...] = prev * w_ref[0:1, :] + u * w_ref[1:2, :] + nxt * w_ref[2:3, :] + b_ref[...]


def short_conv(proj3, conv_w, conv_b):
    b, s, _ = proj3.shape
    nblk = HY_PROJ // LANES
    return pl.pallas_call(
        _short_conv_kernel,
        grid=(b, nblk),
        in_specs=[pl.BlockSpec((None, s, LANES), lambda bi, j: (bi, 0, j)),
                  pl.BlockSpec((3, LANES), lambda bi, j: (0, j)),
                  pl.BlockSpec((1, LANES), lambda bi, j: (0, j))],
        out_specs=pl.BlockSpec((None, s, LANES), lambda bi, j: (bi, 0, j)),
        out_shape=jax.ShapeDtypeStruct((b, s, HY_PROJ), F32),
        compiler_params=_params("parallel", "parallel"),
        name="short_conv",
    )(proj3, conv_w, conv_b.reshape(1, -1))


def _filter_kernel(z_ref, w1_ref, b1_ref, w2_ref, b2_ref, w3_ref, b3_ref, fr_ref, w4_ref, nd_ref,
                   h_ref, asum_ref):
    i = pl.program_id(0)
    hi = lax.Precision.HIGHEST
    z = z_ref[...]
    fr = fr_ref[...]
    h = jnp.sin(fr * (jnp.dot(z, w1_ref[...], precision=hi, preferred_element_type=F32) + b1_ref[...]))
    h = jnp.sin(fr * (jnp.dot(h, w2_ref[...], precision=hi, preferred_element_type=F32) + b2_ref[...]))
    h = jnp.sin(fr * (jnp.dot(h, w3_ref[...], precision=hi, preferred_element_type=F32) + b3_ref[...]))
    hf = jnp.dot(h, w4_ref[...], precision=hi, preferred_element_type=F32)
    decay = jnp.exp(z[:, 0:1] * nd_ref[...])
    tt = z.shape[0]
    row = lax.broadcasted_iota(jnp.int32, (tt, D_HYENA), 0) + i * tt

    @pl.when(i == 0)
    def _():
        asum_ref[...] = jnp.zeros_like(asum_ref)

    for g in range(4):
        sl = slice(g * D_HYENA, (g + 1) * D_HYENA)
        hg = hf[:, sl] * decay
        if g >= 2:
            hg = jnp.where(row == 0, 0.0, hg)
        h_ref[:, sl] = hg
        asum_ref[:, sl] += jnp.sum(jnp.abs(hg), axis=0, keepdims=True)


def filter_taps(z_pad, w1p, b1, w2, b2, w3, b3, fr, w4, neg_abs_delta, tt=512):
    seq = z_pad.shape[0]
    r = FILT_ORDER
    full = lambda shp: pl.BlockSpec(shp, lambda i: (0, 0))
    return pl.pallas_call(
        _filter_kernel,
        grid=(seq // tt,),
        in_specs=[pl.BlockSpec((tt, LANES), lambda i: (i, 0)),
                  full((LANES, r)), full((1, r)), full((r, r)), full((1, r)), full((r, r)), full((1, r)),
                  full((1, r)), full((r, 4 * D_HYENA)), full((1, D_HYENA))],
        out_specs=[pl.BlockSpec((tt, 4 * D_HYENA), lambda i: (i, 0)),
                   pl.BlockSpec((1, 4 * D_HYENA), lambda i: (0, 0))],
        out_shape=[jax.ShapeDtypeStruct((seq, 4 * D_HYENA), F32),
                   jax.ShapeDtypeStruct((1, 4 * D_HYENA), F32)],
        compiler_params=_params("arbitrary"),
        name="hyena_filter",
    )(z_pad, w1p, b1.reshape(1, r), w2, b2.reshape(1, r), w3, b3.reshape(1, r), fr.reshape(1, r), w4,
      neg_abs_delta)


def _stack_c(m):
    return np.block([[m.real, -m.imag], [m.imag, m.real]])


@functools.lru_cache(maxsize=None)
def _dft_tables(seq):
    n = 2 * seq
    r1 = LANES
    r2h = seq // r1
    n1 = np.arange(r1)
    k2 = np.arange(r1)
    n2 = np.arange(r2h)
    e = (k2[None, :, None] * (n1[:, None, None] + r1 * n2[None, None, :])) % n
    t = np.exp(-2j * np.pi * e / n)
    g_fwd = np.concatenate([np.concatenate([t.real, -t.imag], axis=2),
                            np.concatenate([t.imag, t.real], axis=2)], axis=1)
    zero = np.zeros_like(t.real)
    g_filt = np.concatenate([np.concatenate([t.real, zero], axis=2),
                             np.concatenate([t.imag, zero], axis=2),
                             np.concatenate([zero, t.real], axis=2),
                             np.concatenate([zero, -t.imag], axis=2)], axis=1)
    f = np.exp(-2j * np.pi * ((n1[:, None] * n1[None, :]) % r1) / r1)
    m3 = _stack_c(f)
    m3_filt = np.block([[f.real, -f.imag, f.real, f.imag],
                        [f.imag, f.real, -f.imag, f.real]])
    e = (n1[None, :, None] * (r1 * n1[None, None, :] + k2[:, None, None])) % n
    einv = np.exp(2j * np.pi * e / n)
    g_inv = np.concatenate([np.concatenate([einv.real, -einv.imag], axis=2),
                            np.concatenate([einv.imag, einv.real], axis=2)], axis=1)
    d = np.exp(2j * np.pi * ((n2[:, None] * k2[None, :]) % r1) / r1)
    m5 = _stack_c(d)
    tabs = dict(g_fwd=g_fwd, g_filt=g_filt, m3=m3, m3_filt=m3_filt, g_inv=g_inv, m5=m5)
    return {k: v.astype(np.float32) for k, v in tabs.items()}


GRP = 8


def _filt_stage1_kernel(xf_ref, xb_ref, sf_ref, sb_ref, g_ref, o_ref):
    scale = 1.0 / (sf_ref[...] + sb_ref[...])
    r1 = o_ref.shape[2]
    for j in range(GRP):
        zin = jnp.concatenate([xf_ref[:, j, :], xb_ref[:, j, :]], axis=0) * scale
        y = jnp.dot(g_ref[j], zin.astype(BF16), preferred_element_type=F32)
        for p in range(4):
            o_ref[p, j] = y[p * r1:(p + 1) * r1]


def _filt_stage2_kernel(b_ref, m_ref, o_ref, *, inv_n):
    r1 = o_ref.shape[2]
    for j in range(GRP):
        zin = jnp.concatenate([b_ref[p, :, j, :] for p in range(4)], axis=0).astype(BF16)
        x = jnp.dot(m_ref[...], zin, preferred_element_type=F32) * inv_n
        o_ref[0, j] = x[:r1]
        o_ref[1, j] = x[r1:]


def filter_spectrum(taps, asum, tabs, seq):
    c = D_HYENA
    r1 = LANES
    r2h = seq // r1
    x4 = taps.reshape(r2h, r1, 4 * c)
    s1 = pl.pallas_call(
        _filt_stage1_kernel,
        grid=(2, r1 // GRP),
        in_specs=[pl.BlockSpec((r2h, GRP, c), lambda o, i: (0, i, o)),
                  pl.BlockSpec((r2h, GRP, c), lambda o, i: (0, i, 2 + o)),
                  pl.BlockSpec((1, c), lambda o, i: (0, o)),
                  pl.BlockSpec((1, c), lambda o, i: (0, 2 + o)),
                  pl.BlockSpec((GRP, 4 * r1, 2 * r2h), lambda o, i: (i, 0, 0))],
        out_specs=pl.BlockSpec((None, 4, GRP, r1, c), lambda o, i: (o, 0, i, 0, 0)),
        out_shape=jax.ShapeDtypeStruct((2, 4, r1, r1, c), F32),
        compiler_params=_params("parallel", "parallel"),
        name="filt_dft1",
    )(x4, x4, asum, asum, tabs["g_filt"])
    return pl.pallas_call(
        functools.partial(_filt_stage2_kernel, inv_n=1.0 / (2 * seq)),
        grid=(2, r1 // GRP),
        in_specs=[pl.BlockSpec((None, 4, r1, GRP, c), lambda o, i: (o, 0, 0, i, 0)),
                  pl.BlockSpec((2 * r1, 4 * r1), lambda o, i: (0, 0))],
        out_specs=pl.BlockSpec((None, 2, GRP, r1, c), lambda o, i: (o, 0, i, 0, 0)),
        out_shape=jax.ShapeDtypeStruct((2, 2, r1, r1, c), F32),
        compiler_params=_params("parallel", "parallel"),
        name="filt_dft2",
    )(s1, tabs["m3_filt"])


def _conv_stage1_kernel(x_ref, g_ref, o_ref):
    r1 = o_ref.shape[2]
    for j in range(GRP):
        zin = jnp.concatenate([x_ref[0, :, j, :], x_ref[1, :, j, :]], axis=0).astype(BF16)
        y = jnp.dot(g_ref[j], zin, preferred_element_type=F32)
        o_ref[0, j] = y[:r1]
        o_ref[1, j] = y[r1:]


def _conv_stage2_kernel(b_ref, kf_ref, m3_ref, ginv_ref, o_ref):
    r1 = o_ref.shape[2]
    for j in range(GRP):
        zin = jnp.concatenate([b_ref[0, :, j, :], b_ref[1, :, j, :]], axis=0).astype(BF16)
        x = jnp.dot(m3_ref[...], zin, preferred_element_type=F32)
        xr, xi = x[:r1], x[r1:]
        kr, ki = kf_ref[0, j], kf_ref[1, j]
        y = jnp.concatenate([xr * kr - xi * ki, xr * ki + xi * kr], axis=0).astype(BF16)
        q = jnp.dot(ginv_ref[j], y, preferred_element_type=F32)
        o_ref[0, j] = q[:r1]
        o_ref[1, j] = q[r1:]


def _conv_stage3_mid_kernel(q_ref, m5_ref, u_ref, x1_ref, d_ref, o_ref):
    half = u_ref.shape[1]
    d = d_ref[...]
    for j in range(GRP):
        zin = jnp.concatenate([q_ref[0, :, j, :], q_ref[1, :, j, :]], axis=0).astype(BF16)
        y = jnp.dot(m5_ref[...], zin, preferred_element_type=F32)
        for bi in range(2):
            conv = y[bi * half:(bi + 1) * half] + u_ref[bi, :, j, :] * d
            o_ref[bi, :, j, :] = x1_ref[bi, :, j, :] * conv


def _conv_stage3_last_kernel(q_ref, m5_ref, u_ref, x2_ref, gate_ref, d_ref, o_ref):
    half = u_ref.shape[1]
    d = d_ref[...]
    for j in range(GRP):
        zin = jnp.concatenate([q_ref[0, :, j, :], q_ref[1, :, j, :]], axis=0).astype(BF16)
        y = jnp.dot(m5_ref[...], zin, preferred_element_type=F32)
        for bi in range(2):
            g = gate_ref[bi, :, j, :]
            conv = y[bi * half:(bi + 1) * half] + u_ref[bi, :, j, :] * d
            o_ref[bi, :, j, :] = x2_ref[bi, :, j, :] * conv * (g / (1.0 + jnp.exp(-g)))


def _conv_fwd(src4, blk, kf, order, tabs):
    c = D_HYENA
    r1 = LANES
    b, r2h = src4.shape[0], src4.shape[1]
    steps = r1 // GRP
    s1 = pl.pallas_call(
        _conv_stage1_kernel,
        grid=(steps,),
        in_specs=[pl.BlockSpec((b, r2h, GRP, c), lambda i: (0, 0, i, blk)),
                  pl.BlockSpec((GRP, 2 * r1, 2 * r2h), lambda i: (i, 0, 0))],
        out_specs=pl.BlockSpec((2, GRP, r1, c), lambda i: (0, i, 0, 0)),
        out_shape=jax.ShapeDtypeStruct((2, r1, r1, c), F32),
        compiler_params=_params("parallel"),
        name="conv_dft1",
    )(src4, tabs["g_fwd"])
    return pl.pallas_call(
        _conv_stage2_kernel,
        grid=(steps,),
        in_specs=[pl.BlockSpec((2, r1, GRP, c), lambda i: (0, 0, i, 0)),
                  pl.BlockSpec((None, 2, GRP, r1, c), lambda i: (order, 0, i, 0, 0)),
                  pl.BlockSpec((2 * r1, 2 * r1), lambda i: (0, 0)),
                  pl.BlockSpec((GRP, 2 * r1, 2 * r1), lambda i: (i, 0, 0))],
        out_specs=pl.BlockSpec((2, GRP, r1, c), lambda i: (0, i, 0, 0)),
        out_shape=jax.ShapeDtypeStruct((2, r1, r1, c), F32),
        compiler_params=_params("parallel"),
        name="conv_dft2",
    )(s1, kf, tabs["m3"], tabs["g_inv"])


def hyena_branch(csc, proj3, spec, filt_bias, tabs, seq):
    c = D_HYENA
    r1 = LANES
    r2h = seq // r1
    b = csc.shape[0]
    steps = r1 // GRP
    csc4 = csc.reshape(b, r2h, r1, 3 * c)
    proj4 = proj3.reshape(b, r2h, r1, D_IN)
    qspec = pl.BlockSpec((2, r1, GRP, c), lambda i: (0, 0, i, 0))
    m5spec = pl.BlockSpec((2 * r2h, 2 * r1), lambda i: (0, 0))
    dspec = pl.BlockSpec((1, c), lambda i: (0, 0))
    tblk = lambda blk: pl.BlockSpec((b, r2h, GRP, c), lambda i: (0, 0, i, blk))

    q1 = _conv_fwd(csc4, 0, spec, 0, tabs)
    z = pl.pallas_call(
        _conv_stage3_mid_kernel,
        grid=(steps,),
        in_specs=[qspec, m5spec, tblk(0), tblk(1), dspec],
        out_specs=tblk(0),
        out_shape=jax.ShapeDtypeStruct((b, r2h, r1, c), F32),
        compiler_params=_params("parallel"),
        name="conv_idft_mid",
    )(q1, tabs["m5"], csc4, csc4, filt_bias[0:1])

    q2 = _conv_fwd(z, 0, spec, 1, tabs)
    y = pl.pallas_call(
        _conv_stage3_last_kernel,
        grid=(steps,),
        in_specs=[qspec, m5spec, tblk(0), tblk(2), tblk(HY_PROJ // c), dspec],
        out_specs=tblk(0),
        out_shape=jax.ShapeDtypeStruct((b, r2h, r1, c), F32),
        compiler_params=_params("parallel"),
        name="conv_idft_last",
    )(q2, tabs["m5"], z, csc4, proj4, filt_bias[1:2])
    return y.reshape(b, seq, c)


def _out_proj_kernel(yh_ref, ya_ref, w_ref, h_ref, g_ref, o_ref, *, final):
    k = yh_ref.shape[1]
    acc = jnp.dot(yh_ref[...].astype(BF16), w_ref[:k, :], preferred_element_type=F32)
    acc = acc + jnp.dot(ya_ref[...].astype(BF16), w_ref[k:, :], preferred_element_type=F32)
    hn = h_ref[...] + acc
    if final:
        ms = jnp.mean(hn * hn, axis=-1, keepdims=True)
        hn = hn * lax.rsqrt(ms + NORM_EPS) * g_ref[...]
    o_ref[...] = hn


def out_proj(yh, ya, w_bf16, h2, g, final, tm=512):
    m, d = h2.shape
    k = yh.shape[1]
    return pl.pallas_call(
        functools.partial(_out_proj_kernel, final=final),
        grid=(m // tm,),
        in_specs=[pl.BlockSpec((tm, k), lambda i: (i, 0)),
                  pl.BlockSpec((tm, k), lambda i: (i, 0)),
                  pl.BlockSpec((2 * k, d), lambda i: (0, 0)),
                  pl.BlockSpec((tm, d), lambda i: (i, 0)),
                  pl.BlockSpec((1, d), lambda i: (0, 0))],
        out_specs=pl.BlockSpec((tm, d), lambda i: (i, 0)),
        out_shape=jax.ShapeDtypeStruct((m, d), F32),
        compiler_params=_params("parallel"),
        name="out_proj",
    )(yh, ya, w_bf16, h2, g.reshape(1, d))


def _rope_tables(seq):
    inv_freq = 1.0 / (ROPE_THETA ** (jnp.arange(0, QK_DIM, 2, dtype=F32) / QK_DIM))
    ang = jnp.arange(seq, dtype=F32)[:, None] * inv_freq[None, :]
    ang = jnp.concatenate([ang, ang, ang, ang], axis=-1)
    sign = jnp.where((jnp.arange(LANES) % QK_DIM) < QK_DIM // 2, -1.0, 1.0).astype(F32)
    return jnp.cos(ang), jnp.sin(ang) * sign[None, :]


def _filter_features(seq):
    t = jnp.linspace(0.0, 1.0, seq, dtype=F32)[:, None]
    bands = (EMB_DIM - 1) // 2
    wpos = 2.0 * math.pi * jnp.arange(seq, dtype=F32) / seq
    fb = jnp.linspace(1e-4, bands - 1, bands, dtype=F32)
    ph = wpos[:, None] * fb[None, :]
    z = jnp.concatenate([t, jnp.cos(ph), -jnp.sin(ph)], axis=-1)
    z_pad = jnp.pad(z, ((0, 0), (0, LANES - EMB_DIM)))
    max_decay = math.log(DECAY_TARGET) / FAST_DECAY_PCT
    min_decay = math.log(DECAY_TARGET) / SLOW_DECAY_PCT
    deltas = jnp.linspace(min_decay, max_decay, D_HYENA, dtype=F32)
    return z_pad, (-jnp.abs(deltas))[None, :]


def kernel(x, norm_g, w_in, conv_w, conv_b, filt_w1, filt_b1, filt_w2, filt_b2, filt_w3, filt_b3,
           filt_freq, filt_w4, filt_bias, lam_q1, lam_k1, lam_q2, lam_k2, subln_g, w_out, final_g):
    b, seq, d = x.shape
    depth = w_in.shape[0]
    m = b * seq
    tabs = {k: jnp.asarray(v).astype(BF16) for k, v in _dft_tables(seq).items()}
    cos_t, sin_t = _rope_tables(seq)
    z_pad, neg_abs_delta = _filter_features(seq)

    h = x.reshape(m, d)
    for layer in range(depth):
        lam_init = 0.8 - 0.6 * math.exp(-0.3 * layer)
        proj = in_proj(h, norm_g[layer], w_in[layer].astype(BF16))
        proj3 = proj.reshape(b, seq, D_IN)

        q0, q1, kr, vt = rope(proj, cos_t, sin_t, b, seq)
        sh = (b, seq, D_ATTN)
        y_a = attention(q0.reshape(sh), q1.reshape(sh), kr.reshape(sh), vt, proj3,
                        lam_q1[layer], lam_k1[layer], lam_q2[layer], lam_k2[layer], subln_g[layer], lam_init)

        w1p = jnp.pad(filt_w1[layer], ((0, LANES - EMB_DIM), (0, 0)))
        taps, asum = filter_taps(z_pad, w1p, filt_b1[layer], filt_w2[layer], filt_b2[layer], filt_w3[layer],
                                 filt_b3[layer], filt_freq[layer], filt_w4[layer], neg_abs_delta)
        spec = filter_spectrum(taps, asum, tabs, seq)
        csc = short_conv(proj3, conv_w[layer], conv_b[layer])
        y_h = hyena_branch(csc, proj3, spec, filt_bias[layer], tabs, seq)

        h = out_proj(y_h.reshape(m, D_HYENA), y_a.reshape(m, D_ATTN), w_out[layer].astype(BF16), h,
                     final_g, final=(layer == depth - 1))
    return h.reshape(b, seq, d)
```

```python
import functools
import math

import numpy as np
import jax
import jax.numpy as jnp
from jax import lax
from jax.experimental import pallas as pl
from jax.experimental.pallas import tpu as pltpu

F32 = jnp.float32
BF16 = jnp.bfloat16

D_MODEL = 1024
D_HYENA = 512
D_ATTN = 512
N_HEADS = 4
V_DIM = 128
QK_DIM = 64
HY_PROJ = 3 * D_HYENA
D_IN = 4096
EMB_DIM = 33
FILT_ORDER = 64
DECAY_TARGET = 1e-2
FAST_DECAY_PCT = 0.3
SLOW_DECAY_PCT = 1.5
ROPE_THETA = 10000.0
NORM_EPS = 1e-6
SUBLN_EPS = 1e-5

LANES = 128
VMEM_LIMIT = 48 * 1024 * 1024


def _params(*sem):
    return pltpu.CompilerParams(dimension_semantics=sem, vmem_limit_bytes=VMEM_LIMIT)


def _in_proj_kernel(x_ref, g_ref, w_ref, o_ref, u_ref):
    @pl.when(pl.program_id(1) == 0)
    def _():
        x = x_ref[...]
        ms = jnp.mean(x * x, axis=-1, keepdims=True)
        u_ref[...] = (x * lax.rsqrt(ms + NORM_EPS) * g_ref[...]).astype(BF16)

    o_ref[...] = jnp.dot(u_ref[...], w_ref[...], preferred_element_type=F32)


def in_proj(x2, g, w_bf16, tm=1024, tn=1024):
    m, d = x2.shape
    n = w_bf16.shape[1]
    return pl.pallas_call(
        _in_proj_kernel,
        grid=(m // tm, n // tn),
        in_specs=[pl.BlockSpec((tm, d), lambda i, j: (i, 0)),
                  pl.BlockSpec((1, d), lambda i, j: (0, 0)),
                  pl.BlockSpec((d, tn), lambda i, j: (0, j))],
        out_specs=pl.BlockSpec((tm, tn), lambda i, j: (i, j)),
        out_shape=jax.ShapeDtypeStruct((m, n), F32),
        scratch_shapes=[pltpu.VMEM((tm, d), BF16)],
        compiler_params=_params("parallel", "arbitrary"),
        name="in_proj",
    )(x2, g.reshape(1, d), w_bf16)


LOG2E = 1.4426950408889634


def _rope_kernel(q_ref, k_ref, v_ref, cos_ref, sin_ref, q0_ref, q1_ref, ko_ref, vt_ref):
    cos = cos_ref[...]
    sin = sin_ref[...]
    lane = lax.broadcasted_iota(jnp.int32, cos.shape, 1)
    first = (lane & (QK_DIM - 1)) < QK_DIM // 2
    map0 = lane < QK_DIM
    for h in range(N_HEADS):
        sl = slice(h * LANES, (h + 1) * LANES)
        for src, is_q in ((q_ref, True), (k_ref, False)):
            t = src[:, sl]
            rot = jnp.where(first, pltpu.roll(t, LANES - QK_DIM // 2, axis=1),
                            pltpu.roll(t, QK_DIM // 2, axis=1))
            r = t * cos + rot * sin
            if is_q:
                r = r * (QK_DIM ** -0.5 * LOG2E)
                q0_ref[:, sl] = jnp.where(map0, r, 0.0).astype(BF16)
                q1_ref[:, sl] = jnp.where(map0, 0.0, r).astype(BF16)
            else:
                ko_ref[:, sl] = r.astype(BF16)
        vt_ref[h] = v_ref[:, sl].T.astype(BF16)


def rope(proj, cos_t, sin_t, batch, seq, tm=512):
    m = proj.shape[0]
    w = N_HEADS * LANES
    nq = D_IN // w
    spb = seq // tm
    out = jax.ShapeDtypeStruct((m, w), BF16)
    ospec = pl.BlockSpec((tm, w), lambda i: (i, 0))
    return pl.pallas_call(
        _rope_kernel,
        grid=(m // tm,),
        in_specs=[pl.BlockSpec((tm, w), lambda i: (i, nq - 4)),
                  pl.BlockSpec((tm, w), lambda i: (i, nq - 3)),
                  pl.BlockSpec((tm, w), lambda i: (i, nq - 2)),
                  pl.BlockSpec((tm, LANES), lambda i: (i % spb, 0)),
                  pl.BlockSpec((tm, LANES), lambda i: (i % spb, 0))],
        out_specs=[ospec, ospec, ospec,
                   pl.BlockSpec((None, N_HEADS, None, V_DIM, tm), lambda i: (i // spb, 0, i % spb, 0, 0))],
        out_shape=[out, out, out, jax.ShapeDtypeStruct((batch, N_HEADS, spb, V_DIM, tm), BF16)],
        compiler_params=_params("parallel"),
        name="rope",
    )(proj, proj, proj, cos_t, sin_t)


def _attn_kernel(q0_ref, q1_ref, k_ref, vt_ref, gate_ref, lq1_ref, lk1_ref, lq2_ref, lk2_ref,
                 subg_ref, o_ref, *, lam_init):
    tq = q0_ref.shape[0]
    nk, _, tk = vt_ref.shape
    q0 = q0_ref[...]
    q1 = q1_ref[...]

    def scores(q, kj):
        return lax.dot_general(kj, q, (((1,), (1,)), ((), ())), preferred_element_type=F32)

    def soft(st, m, l):
        m_new = jnp.maximum(m, jnp.max(st, axis=0, keepdims=True))
        alpha = jnp.exp2(m - m_new)
        p = jnp.exp2(st - m_new)
        l = alpha * l + jnp.sum(p, axis=0, keepdims=True)
        return m_new, l, alpha, p.astype(BF16)

    neg = jnp.full((1, tq), -1e30, F32)
    zero1 = jnp.zeros((1, tq), F32)
    zacc = jnp.zeros((V_DIM, tq), F32)
    m0, l0, a0, m1, l1, a1 = neg, zero1, zacc, neg, zero1, zacc
    k0 = k_ref[0:tk, :]
    nxt = (scores(q0, k0), scores(q1, k0))
    for j in range(nk):
        st0, st1 = nxt
        if j + 1 < nk:
            kn = k_ref[(j + 1) * tk:(j + 2) * tk, :]
            nxt = (scores(q0, kn), scores(q1, kn))
        vtj = vt_ref[j]
        m0, l0, al0, p0 = soft(st0, m0, l0)
        m1, l1, al1, p1 = soft(st1, m1, l1)
        a0 = al0 * a0 + jnp.dot(vtj, p0, preferred_element_type=F32)
        a1 = al1 * a1 + jnp.dot(vtj, p1, preferred_element_type=F32)

    lam = (jnp.exp(jnp.sum(lq1_ref[...] * lk1_ref[...], axis=-1, keepdims=True))
           - jnp.exp(jnp.sum(lq2_ref[...] * lk2_ref[...], axis=-1, keepdims=True)) + lam_init)
    ot = a0 / l0 - lam * (a1 / l1)
    ot = ot * lax.rsqrt(jnp.mean(ot * ot, axis=0, keepdims=True) + SUBLN_EPS)
    o = ot.T * subg_ref[...] * (1.0 - lam_init)
    g = gate_ref[...]
    o_ref[...] = (o * (g / (1.0 + jnp.exp(-g)))).astype(o_ref.dtype)


def attention(q0, q1, k, vt, proj3, lq1, lk1, lq2, lk2, subg, lam_init, tq=512):
    b, s, _ = q0.shape
    nk, tk = vt.shape[2], vt.shape[4]
    gate_blk = (D_IN - D_ATTN) // LANES
    qspec = pl.BlockSpec((None, tq, LANES), lambda bi, h, qi: (bi, qi, h))
    small = pl.BlockSpec((1, QK_DIM), lambda bi, h, qi: (0, 0))
    return pl.pallas_call(
        functools.partial(_attn_kernel, lam_init=lam_init),
        grid=(b, N_HEADS, s // tq),
        in_specs=[qspec, qspec,
                  pl.BlockSpec((None, s, LANES), lambda bi, h, qi: (bi, 0, h)),
                  pl.BlockSpec((None, None, nk, V_DIM, tk), lambda bi, h, qi: (bi, h, 0, 0, 0)),
                  pl.BlockSpec((None, tq, LANES), lambda bi, h, qi: (bi, qi, gate_blk + h)),
                  small, small, small, small,
                  pl.BlockSpec((1, V_DIM), lambda bi, h, qi: (0, 0))],
        out_specs=pl.BlockSpec((None, tq, LANES), lambda bi, h, qi: (bi, qi, h)),
        out_shape=jax.ShapeDtypeStruct((b, s, D_ATTN), BF16),
        compiler_params=_params("parallel", "parallel", "arbitrary"),
        name="diff_attn",
    )(q0, q1, k, vt, proj3, lq1.reshape(1, -1), lk1.reshape(1, -1), lq2.reshape(1, -1),
      lk2.reshape(1, -1), subg.reshape(1, -1))


def _short_conv_kernel(u_ref, w_ref, b_ref, o_ref):
    u = u_ref[...]
    n = u.shape[0]
    row = lax.broadcasted_iota(jnp.int32, u.shape, 0)
    prev = jnp.where(row == 0, 0.0, pltpu.roll(u, 1, axis=0))
    nxt = jnp.where(row == n - 1, 0.0, pltpu.roll(u, n - 1, axis=0))
    o_ref[...] = prev * w_ref[0:1, :] + u * w_ref[1:2, :] + nxt * w_ref[2:3, :] + b_ref[...]


def short_conv(proj3, conv_w, conv_b):
    b, s, _ = proj3.shape
    nblk = HY_PROJ // LANES
    return pl.pallas_call(
        _short_conv_kernel,
        grid=(b, nblk),
        in_specs=[pl.BlockSpec((None, s, LANES), lambda bi, j: (bi, 0, j)),
                  pl.BlockSpec((3, LANES), lambda bi, j: (0, j)),
                  pl.BlockSpec((1, LANES), lambda bi, j: (0, j))],
        out_specs=pl.BlockSpec((None, s, LANES), lambda bi, j: (bi, 0, j)),
        out_shape=jax.ShapeDtypeStruct((b, s, HY_PROJ), F32),
        compiler_params=_params("parallel", "parallel"),
        name="short_conv",
    )(proj3, conv_w, conv_b.reshape(1, -1))


def _filter_kernel(z_ref, w1_ref, b1_ref, w2_ref, b2_ref, w3_ref, b3_ref, fr_ref, w4_ref, nd_ref,
                   k_ref, asum_ref, *, half_steps):
    i = pl.program_id(0)
    hi = lax.Precision.HIGHEST
    z = z_ref[...]
    fr = fr_ref[...]
    h = jnp.sin(fr * (jnp.dot(z, w1_ref[...], precision=hi, preferred_element_type=F32) + b1_ref[...]))
    h = jnp.sin(fr * (jnp.dot(h, w2_ref[...], precision=hi, preferred_element_type=F32) + b2_ref[...]))
    h = jnp.sin(fr * (jnp.dot(h, w3_ref[...], precision=hi, preferred_element_type=F32) + b3_ref[...]))
    hf = jnp.dot(h.astype(BF16), w4_ref[...].astype(BF16), preferred_element_type=F32)
    decay = jnp.exp(z[:, 0:1] * nd_ref[...])
    tt = z.shape[0]
    row = lax.broadcasted_iota(jnp.int32, (tt, D_HYENA), 0)
    centre = jnp.logical_and(row == 0, i == half_steps)

    @pl.when(i == 0)
    def _():
        asum_ref[...] = jnp.zeros_like(asum_ref)

    for o in range(2):
        sl = slice(o * D_HYENA, (o + 1) * D_HYENA)
        kg = jnp.where(centre, 0.0, hf[:, sl] * decay)
        k_ref[:, sl] = kg
        asum_ref[:, sl] += jnp.sum(jnp.abs(kg), axis=0, keepdims=True)


def filter_taps(z2_pad, w1p, b1, w2, b2, w3, b3, fr, w4, neg_abs_delta, tt=512):
    n = z2_pad.shape[0]
    r = FILT_ORDER
    steps = n // tt
    full = lambda shp: pl.BlockSpec(shp, lambda i: (0, 0))
    return pl.pallas_call(
        functools.partial(_filter_kernel, half_steps=steps // 2),
        grid=(steps,),
        in_specs=[pl.BlockSpec((tt, LANES), lambda i: (i, 0)),
                  full((LANES, r)), full((1, r)), full((r, r)), full((1, r)), full((r, r)), full((1, r)),
                  full((1, r)),
                  pl.BlockSpec((r, 2 * D_HYENA), lambda i: (0, i // (steps // 2))),
                  full((1, D_HYENA))],
        out_specs=[pl.BlockSpec((tt, 2 * D_HYENA), lambda i: (i, 0)),
                   pl.BlockSpec((1, 2 * D_HYENA), lambda i: (0, 0))],
        out_shape=[jax.ShapeDtypeStruct((n, 2 * D_HYENA), F32),
                   jax.ShapeDtypeStruct((1, 2 * D_HYENA), F32)],
        compiler_params=_params("arbitrary"),
        name="hyena_filter",
    )(z2_pad, w1p, b1.reshape(1, r), w2, b2.reshape(1, r), w3, b3.reshape(1, r), fr.reshape(1, r), w4,
      neg_abs_delta)


def _stack_c(m):
    return np.block([[m.real, -m.imag], [m.imag, m.real]])


@functools.lru_cache(maxsize=None)
def _dft_tables(seq):
    n = 2 * seq
    r1 = LANES
    r2h = seq // r1
    n1 = np.arange(r1)
    k2 = np.arange(r1)
    n2 = np.arange(r2h)
    e = (k2[None, :, None] * (n1[:, None, None] + r1 * n2[None, None, :])) % n
    t = np.exp(-2j * np.pi * e / n)
    g_fwd = np.concatenate([np.concatenate([t.real, -t.imag], axis=2),
                            np.concatenate([t.imag, t.real], axis=2)], axis=1)
    n2f = np.arange(2 * r2h)
    e = (k2[None, :, None] * (n1[:, None, None] + r1 * n2f[None, None, :])) % n
    tf = np.exp(-2j * np.pi * e / n)
    g_real = np.concatenate([tf.real, tf.imag], axis=1)
    f = np.exp(-2j * np.pi * ((n1[:, None] * n1[None, :]) % r1) / r1)
    m3 = _stack_c(f)
    e = (n1[None, :, None] * (r1 * n1[None, None, :] + k2[:, None, None])) % n
    einv = np.exp(2j * np.pi * e / n)
    g_inv = np.concatenate([np.concatenate([einv.real, -einv.imag], axis=2),
                            np.concatenate([einv.imag, einv.real], axis=2)], axis=1)
    d = np.exp(2j * np.pi * ((n2[:, None] * k2[None, :]) % r1) / r1)
    m5 = _stack_c(d)
    tabs = dict(g_fwd=g_fwd, g_real=g_real, m3=m3, g_inv=g_inv, m5=m5)
    return {k: v.astype(np.float32) for k, v in tabs.items()}


GRP = 8


def _filt_stage1_kernel(x_ref, s_ref, g_ref, o_ref):
    scale = 1.0 / s_ref[...]
    r1 = o_ref.shape[2]
    for j in range(GRP):
        zin = (x_ref[:, j, :] * scale).astype(BF16)
        y = jnp.dot(g_ref[j], zin, preferred_element_type=F32)
        o_ref[0, j] = y[:r1]
        o_ref[1, j] = y[r1:]


def _filt_stage2_kernel(b_ref, m_ref, o_ref, *, inv_n):
    r1 = o_ref.shape[2]
    for j in range(GRP):
        zin = jnp.concatenate([b_ref[0, :, j, :], b_ref[1, :, j, :]], axis=0).astype(BF16)
        x = jnp.dot(m_ref[...], zin, preferred_element_type=F32) * inv_n
        o_ref[0, j] = x[:r1]
        o_ref[1, j] = x[r1:]


def filter_spectrum(kern, asum, tabs):
    c = D_HYENA
    r1 = LANES
    r2 = kern.shape[0] // r1
    x3 = kern.reshape(r2, r1, 2 * c)
    s1 = pl.pallas_call(
        _filt_stage1_kernel,
        grid=(2, r1 // GRP),
        in_specs=[pl.BlockSpec((r2, GRP, c), lambda o, i: (0, i, o)),
                  pl.BlockSpec((1, c), lambda o, i: (0, o)),
                  pl.BlockSpec((GRP, 2 * r1, r2), lambda o, i: (i, 0, 0))],
        out_specs=pl.BlockSpec((None, 2, GRP, r1, c), lambda o, i: (o, 0, i, 0, 0)),
        out_shape=jax.ShapeDtypeStruct((2, 2, r1, r1, c), F32),
        compiler_params=_params("parallel", "parallel"),
        name="filt_dft1",
    )(x3, asum, tabs["g_real"])
    return pl.pallas_call(
        functools.partial(_filt_stage2_kernel, inv_n=1.0 / kern.shape[0]),
        grid=(2, r1 // GRP),
        in_specs=[pl.BlockSpec((None, 2, r1, GRP, c), lambda o, i: (o, 0, 0, i, 0)),
                  pl.BlockSpec((2 * r1, 2 * r1), lambda o, i: (0, 0))],
        out_specs=pl.BlockSpec((None, 2, GRP, r1, c), lambda o, i: (o, 0, i, 0, 0)),
        out_shape=jax.ShapeDtypeStruct((2, 2, r1, r1, c), F32),
        compiler_params=_params("parallel", "parallel"),
        name="filt_dft2",
    )(s1, tabs["m3"])


def _conv_stage1_kernel(x_ref, g_ref, o_ref):
    r1 = o_ref.shape[2]
    for j in range(GRP):
        zin = jnp.concatenate([x_ref[0, :, j, :], x_ref[1, :, j, :]], axis=0).astype(BF16)
        y = jnp.dot(g_ref[j], zin, preferred_element_type=F32)
        o_ref[0, j] = y[:r1]
        o_ref[1, j] = y[r1:]


def _conv_stage2_kernel(b_ref, kf_ref, m3_ref, ginv_ref, o_ref):
    r1 = o_ref.shape[2]
    for j in range(GRP):
        zin = jnp.concatenate([b_ref[0, :, j, :], b_ref[1, :, j, :]], axis=0).astype(BF16)
        x = jnp.dot(m3_ref[...], zin, preferred_element_type=F32)
        xr, xi = x[:r1], x[r1:]
        kr, ki = kf_ref[0, j], kf_ref[1, j]
        y = jnp.concatenate([xr * kr - xi * ki, xr * ki + xi * kr], axis=0).astype(BF16)
        q = jnp.dot(ginv_ref[j], y, preferred_element_type=F32)
        o_ref[0, j] = q[:r1]
        o_ref[1, j] = q[r1:]


def _conv_stage3_mid_kernel(q_ref, m5_ref, u_ref, x1_ref, d_ref, o_ref):
    half = u_ref.shape[1]
    d = d_ref[...]
    for j in range(GRP):
        zin = jnp.concatenate([q_ref[0, :, j, :], q_ref[1, :, j, :]], axis=0).astype(BF16)
        y = jnp.dot(m5_ref[...], zin, preferred_element_type=F32)
        for bi in range(2):
            conv = y[bi * half:(bi + 1) * half] + u_ref[bi, :, j, :] * d
            o_ref[bi, :, j, :] = x1_ref[bi, :, j, :] * conv


def _conv_stage3_last_kernel(q_ref, m5_ref, u_ref, x2_ref, gate_ref, d_ref, o_ref):
    half = u_ref.shape[1]
    d = d_ref[...]
    for j in range(GRP):
        zin = jnp.concatenate([q_ref[0, :, j, :], q_ref[1, :, j, :]], axis=0).astype(BF16)
        y = jnp.dot(m5_ref[...], zin, preferred_element_type=F32)
        for bi in range(2):
            g = gate_ref[bi, :, j, :]
            conv = y[bi * half:(bi + 1) * half] + u_ref[bi, :, j, :] * d
            o_ref[bi, :, j, :] = x2_ref[bi, :, j, :] * conv * (g / (1.0 + jnp.exp(-g)))


def _conv_fwd(src4, blk, kf, order, tabs):
    c = D_HYENA
    r1 = LANES
    b, r2h = src4.shape[0], src4.shape[1]
    steps = r1 // GRP
    s1 = pl.pallas_call(
        _conv_stage1_kernel,
        grid=(steps,),
        in_specs=[pl.BlockSpec((b, r2h, GRP, c), lambda i: (0, 0, i, blk)),
                  pl.BlockSpec((GRP, 2 * r1, 2 * r2h), lambda i: (i, 0, 0))],
        out_specs=pl.BlockSpec((2, GRP, r1, c), lambda i: (0, i, 0, 0)),
        out_shape=jax.ShapeDtypeStruct((2, r1, r1, c), F32),
        compiler_params=_params("parallel"),
        name="conv_dft1",
    )(src4, tabs["g_fwd"])
    return pl.pallas_call(
        _conv_stage2_kernel,
        grid=(steps,),
        in_specs=[pl.BlockSpec((2, r1, GRP, c), lambda i: (0, 0, i, 0)),
                  pl.BlockSpec((None, 2, GRP, r1, c), lambda i: (order, 0, i, 0, 0)),
                  pl.BlockSpec((2 * r1, 2 * r1), lambda i: (0, 0)),
                  pl.BlockSpec((GRP, 2 * r1, 2 * r1), lambda i: (i, 0, 0))],
        out_specs=pl.BlockSpec((2, GRP, r1, c), lambda i: (0, i, 0, 0)),
        out_shape=jax.ShapeDtypeStruct((2, r1, r1, c), F32),
        compiler_params=_params("parallel"),
        name="conv_dft2",
    )(s1, kf, tabs["m3"], tabs["g_inv"])


def hyena_branch(csc, proj3, spec, filt_bias, tabs, seq):
    c = D_HYENA
    r1 = LANES
    r2h = seq // r1
    b = csc.shape[0]
    steps = r1 // GRP
    csc4 = csc.reshape(b, r2h, r1, 3 * c)
    proj4 = proj3.reshape(b, r2h, r1, D_IN)
    qspec = pl.BlockSpec((2, r1, GRP, c), lambda i: (0, 0, i, 0))
    m5spec = pl.BlockSpec((2 * r2h, 2 * r1), lambda i: (0, 0))
    dspec = pl.BlockSpec((1, c), lambda i: (0, 0))
    tblk = lambda blk: pl.BlockSpec((b, r2h, GRP, c), lambda i: (0, 0, i, blk))

    q1 = _conv_fwd(csc4, 0, spec, 0, tabs)
    z = pl.pallas_call(
        _conv_stage3_mid_kernel,
        grid=(steps,),
        in_specs=[qspec, m5spec, tblk(0), tblk(1), dspec],
        out_specs=tblk(0),
        out_shape=jax.ShapeDtypeStruct((b, r2h, r1, c), F32),
        compiler_params=_params("parallel"),
        name="conv_idft_mid",
    )(q1, tabs["m5"], csc4, csc4, filt_bias[0:1])

    q2 = _conv_fwd(z, 0, spec, 1, tabs)
    y = pl.pallas_call(
        _conv_stage3_last_kernel,
        grid=(steps,),
        in_specs=[qspec, m5spec, tblk(0), tblk(2), tblk(HY_PROJ // c), dspec],
        out_specs=tblk(0),
        out_shape=jax.ShapeDtypeStruct((b, r2h, r1, c), F32),
        compiler_params=_params("parallel"),
        name="conv_idft_last",
    )(q2, tabs["m5"], z, csc4, proj4, filt_bias[1:2])
    return y.reshape(b, seq, c)


def _out_proj_kernel(yh_ref, ya_ref, w_ref, h_ref, g_ref, o_ref, *, final):
    k = yh_ref.shape[1]
    acc = jnp.dot(yh_ref[...].astype(BF16), w_ref[:k, :], preferred_element_type=F32)
    acc = acc + jnp.dot(ya_ref[...].astype(BF16), w_ref[k:, :], preferred_element_type=F32)
    hn = h_ref[...] + acc
    if final:
        ms = jnp.mean(hn * hn, axis=-1, keepdims=True)
        hn = hn * lax.rsqrt(ms + NORM_EPS) * g_ref[...]
    o_ref[...] = hn


def out_proj(yh, ya, w_bf16, h2, g, final, tm=512):
    m, d = h2.shape
    k = yh.shape[1]
    return pl.pallas_call(
        functools.partial(_out_proj_kernel, final=final),
        grid=(m // tm,),
        in_specs=[pl.BlockSpec((tm, k), lambda i: (i, 0)),
                  pl.BlockSpec((tm, k), lambda i: (i, 0)),
                  pl.BlockSpec((2 * k, d), lambda i: (0, 0)),
                  pl.BlockSpec((tm, d), lambda i: (i, 0)),
                  pl.BlockSpec((1, d), lambda i: (0, 0))],
        out_specs=pl.BlockSpec((tm, d), lambda i: (i, 0)),
        out_shape=jax.ShapeDtypeStruct((m, d), F32),
        compiler_params=_params("parallel"),
        name="out_proj",
    )(yh, ya, w_bf16, h2, g.reshape(1, d))


def _rope_tables(seq):
    inv_freq = 1.0 / (ROPE_THETA ** (jnp.arange(0, QK_DIM, 2, dtype=F32) / QK_DIM))
    ang = jnp.arange(seq, dtype=F32)[:, None] * inv_freq[None, :]
    ang = jnp.concatenate([ang, ang, ang, ang], axis=-1)
    sign = jnp.where((jnp.arange(LANES) % QK_DIM) < QK_DIM // 2, -1.0, 1.0).astype(F32)
    return jnp.cos(ang), jnp.sin(ang) * sign[None, :]


def _filter_features(seq):
    t = jnp.linspace(0.0, 1.0, seq, dtype=F32)[:, None]
    bands = (EMB_DIM - 1) // 2
    wpos = 2.0 * math.pi * jnp.arange(seq, dtype=F32) / seq
    fb = jnp.linspace(1e-4, bands - 1, bands, dtype=F32)
    ph = wpos[:, None] * fb[None, :]
    z = jnp.concatenate([t, jnp.cos(ph), -jnp.sin(ph)], axis=-1)
    z2 = jnp.concatenate([z, jnp.roll(z[::-1], 1, axis=0)], axis=0)
    z_pad = jnp.pad(z2, ((0, 0), (0, LANES - EMB_DIM)))
    max_decay = math.log(DECAY_TARGET) / FAST_DECAY_PCT
    min_decay = math.log(DECAY_TARGET) / SLOW_DECAY_PCT
    deltas = jnp.linspace(min_decay, max_decay, D_HYENA, dtype=F32)
    return z_pad, (-jnp.abs(deltas))[None, :]


def kernel(x, norm_g, w_in, conv_w, conv_b, filt_w1, filt_b1, filt_w2, filt_b2, filt_w3, filt_b3,
           filt_freq, filt_w4, filt_bias, lam_q1, lam_k1, lam_q2, lam_k2, subln_g, w_out, final_g):
    b, seq, d = x.shape
    depth = w_in.shape[0]
    m = b * seq
    tabs = {k: jnp.asarray(v).astype(BF16) for k, v in _dft_tables(seq).items()}
    cos_t, sin_t = _rope_tables(seq)
    z_pad, neg_abs_delta = _filter_features(seq)

    h = x.reshape(m, d)
    for layer in range(depth):
        lam_init = 0.8 - 0.6 * math.exp(-0.3 * layer)
        proj = in_proj(h, norm_g[layer], w_in[layer].astype(BF16))
        proj3 = proj.reshape(b, seq, D_IN)

        q0, q1, kr, vt = rope(proj, cos_t, sin_t, b, seq)
        sh = (b, seq, D_ATTN)
        y_a = attention(q0.reshape(sh), q1.reshape(sh), kr.reshape(sh), vt, proj3,
                        lam_q1[layer], lam_k1[layer], lam_q2[layer], lam_k2[layer], subln_g[layer], lam_init)

        w1p = jnp.pad(filt_w1[layer], ((0, LANES - EMB_DIM), (0, 0)))
        kern, asum = filter_taps(z_pad, w1p, filt_b1[layer], filt_w2[layer], filt_b2[layer], filt_w3[layer],
                                 filt_b3[layer], filt_freq[layer], filt_w4[layer], neg_abs_delta)
        spec = filter_spectrum(kern, asum, tabs)
        csc = short_conv(proj3, conv_w[layer], conv_b[layer])
        y_h = hyena_branch(csc, proj3, spec, filt_bias[layer], tabs, seq)

        h = out_proj(y_h.reshape(m, D_HYENA), y_a.reshape(m, D_ATTN), w_out[layer].astype(BF16), h,
                     final_g, final=(layer == depth - 1))
    return h.reshape(b, seq, d)
```

```python
import functools
import math

import numpy as np
import jax
import jax.numpy as jnp
from jax import lax
from jax.experimental import pallas as pl
from jax.experimental.pallas import tpu as pltpu

F32 = jnp.float32
BF16 = jnp.bfloat16

D_MODEL = 1024
D_HYENA = 512
D_ATTN = 512
N_HEADS = 4
V_DIM = 128
QK_DIM = 64
HY_PROJ = 3 * D_HYENA
D_IN = 4096
EMB_DIM = 33
FILT_ORDER = 64
DECAY_TARGET = 1e-2
FAST_DECAY_PCT = 0.3
SLOW_DECAY_PCT = 1.5
ROPE_THETA = 10000.0
NORM_EPS = 1e-6
SUBLN_EPS = 1e-5

LANES = 128
VMEM_LIMIT = 48 * 1024 * 1024


def _params(*sem):
    return pltpu.CompilerParams(dimension_semantics=sem, vmem_limit_bytes=VMEM_LIMIT)


def _in_proj_kernel(x_ref, g_ref, w_ref, o_ref, u_ref):
    @pl.when(pl.program_id(1) == 0)
    def _():
        x = x_ref[...]
        ms = jnp.mean(x * x, axis=-1, keepdims=True)
        u_ref[...] = (x * lax.rsqrt(ms + NORM_EPS) * g_ref[...]).astype(BF16)

    o_ref[...] = jnp.dot(u_ref[...], w_ref[...], preferred_element_type=F32)


def in_proj(x2, g, w_bf16, tm=1024, tn=1024):
    m, d = x2.shape
    n = w_bf16.shape[1]
    return pl.pallas_call(
        _in_proj_kernel,
        grid=(m // tm, n // tn),
        in_specs=[pl.BlockSpec((tm, d), lambda i, j: (i, 0)),
                  pl.BlockSpec((1, d), lambda i, j: (0, 0)),
                  pl.BlockSpec((d, tn), lambda i, j: (0, j))],
        out_specs=pl.BlockSpec((tm, tn), lambda i, j: (i, j)),
        out_shape=jax.ShapeDtypeStruct((m, n), F32),
        scratch_shapes=[pltpu.VMEM((tm, d), BF16)],
        compiler_params=_params("parallel", "arbitrary"),
        name="in_proj",
    )(x2, g.reshape(1, d), w_bf16)


LOG2E = 1.4426950408889634


def _rope_kernel(q_ref, k_ref, v_ref, cos_ref, sin_ref, q0_ref, q1_ref, ko_ref, vt_ref):
    cos = cos_ref[...]
    sin = sin_ref[...]
    lane = lax.broadcasted_iota(jnp.int32, cos.shape, 1)
    first = (lane & (QK_DIM - 1)) < QK_DIM // 2
    map0 = lane < QK_DIM
    for h in range(N_HEADS):
        sl = slice(h * LANES, (h + 1) * LANES)
        for src, is_q in ((q_ref, True), (k_ref, False)):
            t = src[:, sl]
            rot = jnp.where(first, pltpu.roll(t, LANES - QK_DIM // 2, axis=1),
                            pltpu.roll(t, QK_DIM // 2, axis=1))
            r = t * cos + rot * sin
            if is_q:
                r = r * (QK_DIM ** -0.5 * LOG2E)
                q0_ref[:, sl] = jnp.where(map0, r, 0.0).astype(BF16)
                q1_ref[:, sl] = jnp.where(map0, 0.0, r).astype(BF16)
            else:
                ko_ref[:, sl] = r.astype(BF16)
        vt_ref[h] = v_ref[:, sl].T.astype(BF16)


def rope(proj, cos_t, sin_t, batch, seq, tm=512):
    m = proj.shape[0]
    w = N_HEADS * LANES
    nq = D_IN // w
    spb = seq // tm
    out = jax.ShapeDtypeStruct((m, w), BF16)
    ospec = pl.BlockSpec((tm, w), lambda i: (i, 0))
    return pl.pallas_call(
        _rope_kernel,
        grid=(m // tm,),
        in_specs=[pl.BlockSpec((tm, w), lambda i: (i, nq - 4)),
                  pl.BlockSpec((tm, w), lambda i: (i, nq - 3)),
                  pl.BlockSpec((tm, w), lambda i: (i, nq - 2)),
                  pl.BlockSpec((tm, LANES), lambda i: (i % spb, 0)),
                  pl.BlockSpec((tm, LANES), lambda i: (i % spb, 0))],
        out_specs=[ospec, ospec, ospec,
                   pl.BlockSpec((None, N_HEADS, None, V_DIM, tm), lambda i: (i // spb, 0, i % spb, 0, 0))],
        out_shape=[out, out, out, jax.ShapeDtypeStruct((batch, N_HEADS, spb, V_DIM, tm), BF16)],
        compiler_params=_params("parallel"),
        name="rope",
    )(proj, proj, proj, cos_t, sin_t)


def _attn_kernel(q0_ref, q1_ref, k_ref, vt_ref, gate_ref, lq1_ref, lk1_ref, lq2_ref, lk2_ref,
                 subg_ref, o_ref, *, lam_init):
    tq = q0_ref.shape[0]
    nk, _, tk = vt_ref.shape
    q0 = q0_ref[...]
    q1 = q1_ref[...]

    def scores(q, kj):
        return lax.dot_general(kj, q, (((1,), (1,)), ((), ())), preferred_element_type=F32)

    def soft(st, m, l):
        m_new = jnp.maximum(m, jnp.max(st, axis=0, keepdims=True))
        alpha = jnp.exp2(m - m_new)
        p = jnp.exp2(st - m_new)
        l = alpha * l + jnp.sum(p, axis=0, keepdims=True)
        return m_new, l, alpha, p.astype(BF16)

    neg = jnp.full((1, tq), -1e30, F32)
    zero1 = jnp.zeros((1, tq), F32)
    zacc = jnp.zeros((V_DIM, tq), F32)
    m0, l0, a0, m1, l1, a1 = neg, zero1, zacc, neg, zero1, zacc
    k0 = k_ref[0:tk, :]
    nxt = (scores(q0, k0), scores(q1, k0))
    for j in range(nk):
        st0, st1 = nxt
        if j + 1 < nk:
            kn = k_ref[(j + 1) * tk:(j + 2) * tk, :]
            nxt = (scores(q0, kn), scores(q1, kn))
        vtj = vt_ref[j]
        m0, l0, al0, p0 = soft(st0, m0, l0)
        m1, l1, al1, p1 = soft(st1, m1, l1)
        a0 = al0 * a0 + jnp.dot(vtj, p0, preferred_element_type=F32)
        a1 = al1 * a1 + jnp.dot(vtj, p1, preferred_element_type=F32)

    lam = (jnp.exp(jnp.sum(lq1_ref[...] * lk1_ref[...], axis=-1, keepdims=True))
           - jnp.exp(jnp.sum(lq2_ref[...] * lk2_ref[...], axis=-1, keepdims=True)) + lam_init)
    ot = a0 / l0 - lam * (a1 / l1)
    ot = ot * lax.rsqrt(jnp.mean(ot * ot, axis=0, keepdims=True) + SUBLN_EPS)
    o = ot.T * subg_ref[...] * (1.0 - lam_init)
    g = gate_ref[...]
    o_ref[...] = (o * (g / (1.0 + jnp.exp(-g)))).astype(o_ref.dtype)


def attention(q0, q1, k, vt, proj3, lq1, lk1, lq2, lk2, subg, lam_init, tq=512):
    b, s, _ = q0.shape
    nk, tk = vt.shape[2], vt.shape[4]
    gate_blk = (D_IN - D_ATTN) // LANES
    qspec = pl.BlockSpec((None, tq, LANES), lambda bi, h, qi: (bi, qi, h))
    small = pl.BlockSpec((1, QK_DIM), lambda bi, h, qi: (0, 0))
    return pl.pallas_call(
        functools.partial(_attn_kernel, lam_init=lam_init),
        grid=(b, N_HEADS, s // tq),
        in_specs=[qspec, qspec,
                  pl.BlockSpec((None, s, LANES), lambda bi, h, qi: (bi, 0, h)),
                  pl.BlockSpec((None, None, nk, V_DIM, tk), lambda bi, h, qi: (bi, h, 0, 0, 0)),
                  pl.BlockSpec((None, tq, LANES), lambda bi, h, qi: (bi, qi, gate_blk + h)),
                  small, small, small, small,
                  pl.BlockSpec((1, V_DIM), lambda bi, h, qi: (0, 0))],
        out_specs=pl.BlockSpec((None, tq, LANES), lambda bi, h, qi: (bi, qi, h)),
        out_shape=jax.ShapeDtypeStruct((b, s, D_ATTN), BF16),
        compiler_params=_params("parallel", "parallel", "arbitrary"),
        name="diff_attn",
    )(q0, q1, k, vt, proj3, lq1.reshape(1, -1), lk1.reshape(1, -1), lq2.reshape(1, -1),
      lk2.reshape(1, -1), subg.reshape(1, -1))


def _short_conv_kernel(u_ref, w_ref, b_ref, o_ref):
    u = u_ref[...]
    n = u.shape[0]
    row = lax.broadcasted_iota(jnp.int32, u.shape, 0)
    prev = jnp.where(row == 0, 0.0, pltpu.roll(u, 1, axis=0))
    nxt = jnp.where(row == n - 1, 0.0, pltpu.roll(u, n - 1, axis=0))
    o_ref[...] = prev * w_ref[0:1, :] + u * w_ref[1:2, :] + nxt * w_ref[2:3, :] + b_ref[...]


def short_conv(proj3, conv_w, conv_b):
    b, s, _ = proj3.shape
    nblk = HY_PROJ // LANES
    return pl.pallas_call(
        _short_conv_kernel,
        grid=(b, nblk),
        in_specs=[pl.BlockSpec((None, s, LANES), lambda bi, j: (bi, 0, j)),
                  pl.BlockSpec((3, LANES), lambda bi, j: (0, j)),
                  pl.BlockSpec((1, LANES), lambda bi, j: (0, j))],
        out_specs=pl.BlockSpec((None, s, LANES), lambda bi, j: (bi, 0, j)),
        out_shape=jax.ShapeDtypeStruct((b, s, HY_PROJ), F32),
        compiler_params=_params("parallel", "parallel"),
        name="short_conv",
    )(proj3, conv_w, conv_b.reshape(1, -1))


def _filter_kernel(z_ref, w1_ref, b1_ref, w2_ref, b2_ref, w3_ref, b3_ref, fr_ref, w4_ref, nd_ref,
                   k_ref, asum_ref, *, half_steps):
    i = pl.program_id(0)
    hi = lax.Precision.HIGHEST
    z = z_ref[...]
    fr = fr_ref[...]
    h = jnp.sin(fr * (jnp.dot(z, w1_ref[...], precision=hi, preferred_element_type=F32) + b1_ref[...]))
    h = jnp.sin(fr * (jnp.dot(h, w2_ref[...], precision=hi, preferred_element_type=F32) + b2_ref[...]))
    h = jnp.sin(fr * (jnp.dot(h, w3_ref[...], precision=hi, preferred_element_type=F32) + b3_ref[...]))
    hf = jnp.dot(h.astype(BF16), w4_ref[...].astype(BF16), preferred_element_type=F32)
    decay = jnp.exp(z[:, 0:1] * nd_ref[...])
    tt = z.shape[0]
    row = lax.broadcasted_iota(jnp.int32, (tt, D_HYENA), 0)
    centre = jnp.logical_and(row == 0, i == half_steps)

    @pl.when(i == 0)
    def _():
        asum_ref[...] = jnp.zeros_like(asum_ref)

    for layer in range(2):
        for o in range(2):
            sl = slice(o * D_HYENA, (o + 1) * D_HYENA)
            src = slice((2 * layer + o) * D_HYENA, (2 * layer + o + 1) * D_HYENA)
            kg = jnp.where(centre, 0.0, hf[:, src] * decay)
            k_ref[layer, :, sl] = kg
            asum_ref[layer, :, sl] += jnp.sum(jnp.abs(kg), axis=0, keepdims=True)


def filter_taps(z2_pad, w1, b1, w2, b2, w3, b3, fr, w4, neg_abs_delta, tt=512):
    assert w1.shape[0] == 2, "the filter MLP packs exactly two layers into the 128 lanes"
    n = z2_pad.shape[0]
    r = FILT_ORDER
    c2 = 2 * D_HYENA
    steps = n // tt
    zero = jnp.zeros((r, r), F32)
    blockdiag = lambda w: jnp.concatenate([jnp.concatenate([w[0], zero], axis=1),
                                           jnp.concatenate([zero, w[1]], axis=1)], axis=0)
    cat = lambda v: jnp.concatenate([v[0], v[1]], axis=-1).reshape(1, 2 * r)
    w1c = jnp.pad(jnp.concatenate([w1[0], w1[1]], axis=1), ((0, LANES - EMB_DIM), (0, 0)))
    zc = jnp.zeros((r, c2), F32)
    w4c = jnp.stack([jnp.concatenate([jnp.concatenate([w4[0][:, hf * c2:(hf + 1) * c2], zc], axis=1),
                                      jnp.concatenate([zc, w4[1][:, hf * c2:(hf + 1) * c2]], axis=1)], axis=0)
                     for hf in range(2)])
    full = lambda shp: pl.BlockSpec(shp, lambda i: (0, 0))
    return pl.pallas_call(
        functools.partial(_filter_kernel, half_steps=steps // 2),
        grid=(steps,),
        in_specs=[pl.BlockSpec((tt, LANES), lambda i: (i, 0)),
                  full((LANES, 2 * r)), full((1, 2 * r)), full((2 * r, 2 * r)), full((1, 2 * r)),
                  full((2 * r, 2 * r)), full((1, 2 * r)), full((1, 2 * r)),
                  pl.BlockSpec((None, 2 * r, 2 * c2), lambda i: (i // (steps // 2), 0, 0)),
                  full((1, D_HYENA))],
        out_specs=[pl.BlockSpec((2, tt, c2), lambda i: (0, i, 0)),
                   pl.BlockSpec((2, 1, c2), lambda i: (0, 0, 0))],
        out_shape=[jax.ShapeDtypeStruct((2, n, c2), F32),
                   jax.ShapeDtypeStruct((2, 1, c2), F32)],
        compiler_params=_params("arbitrary"),
        name="hyena_filter",
    )(z2_pad, w1c, cat(b1), blockdiag(w2), cat(b2), blockdiag(w3), cat(b3), cat(fr), w4c, neg_abs_delta)


def _stack_c(m):
    return np.block([[m.real, -m.imag], [m.imag, m.real]])


def _interleave(a, axis):
    h = a.shape[axis] // 2
    idx = np.arange(2 * h).reshape(2, h).T.reshape(-1)
    return np.take(a, idx, axis=axis)


@functools.lru_cache(maxsize=None)
def _dft_tables(seq):
    n = 2 * seq
    r1 = LANES
    r2h = seq // r1
    n1 = np.arange(r1)
    k2 = np.arange(r1)
    n2 = np.arange(r2h)
    e = (k2[None, :, None] * (n1[:, None, None] + r1 * n2[None, None, :])) % n
    t = np.exp(-2j * np.pi * e / n)
    g_fwd = np.concatenate([np.concatenate([t.real, -t.imag], axis=2),
                            np.concatenate([t.imag, t.real], axis=2)], axis=1)
    g_fwd = _interleave(g_fwd, 1)
    n2f = np.arange(2 * r2h)
    e = (k2[None, :, None] * (n1[:, None, None] + r1 * n2f[None, None, :])) % n
    tf = np.exp(-2j * np.pi * e / n)
    g_real = _interleave(np.concatenate([tf.real, tf.imag], axis=1), 1)
    f = np.exp(-2j * np.pi * ((n1[:, None] * n1[None, :]) % r1) / r1)
    m3 = _interleave(_stack_c(f), 1)
    e = (n1[None, :, None] * (r1 * n1[None, None, :] + k2[:, None, None])) % n
    einv = np.exp(2j * np.pi * e / n)
    g_inv = np.concatenate([np.concatenate([einv.real, -einv.imag], axis=2),
                            np.concatenate([einv.imag, einv.real], axis=2)], axis=1)
    g_inv = _interleave(g_inv, 1)
    d = np.exp(2j * np.pi * ((n2[:, None] * k2[None, :]) % r1) / r1)
    m5 = _interleave(_stack_c(d), 1)
    tabs = dict(g_fwd=g_fwd, g_real=g_real, m3=m3, g_inv=g_inv, m5=m5)
    return {k: v.astype(np.float32) for k, v in tabs.items()}


def _pack(y):
    return pltpu.bitcast(y.astype(BF16), jnp.uint32)


def _unpack(w):
    return pltpu.bitcast(w, BF16)


GRP = 8


def _filt_stage1_kernel(x_ref, s_ref, g_ref, o_ref):
    scale = 1.0 / s_ref[...]
    half = x_ref.shape[0] // 2
    for j in range(GRP):
        x = jnp.concatenate([x_ref[:half, j, :], x_ref[half:, j, :]], axis=0)
        y = jnp.dot(g_ref[j], (x * scale).astype(BF16), preferred_element_type=F32)
        o_ref[j] = _pack(y)


def _filt_stage2_kernel(b_ref, m_ref, o_ref, *, inv_n):
    r1 = o_ref.shape[2]
    for j in range(GRP):
        x = jnp.dot(m_ref[...], _unpack(b_ref[:, j, :]), preferred_element_type=F32) * inv_n
        o_ref[0, j] = x[:r1].astype(o_ref.dtype)
        o_ref[1, j] = x[r1:].astype(o_ref.dtype)


def filter_spectrum(kern, asum, tabs):
    c = D_HYENA
    r1 = LANES
    nl = kern.shape[0]
    r2 = kern.shape[1] // r1
    x4 = kern.reshape(nl, r2, r1, 2 * c)
    s1 = pl.pallas_call(
        _filt_stage1_kernel,
        grid=(nl, 2, r1 // GRP),
        in_specs=[pl.BlockSpec((None, r2, GRP, c), lambda l, o, i: (l, 0, i, o)),
                  pl.BlockSpec((None, 1, c), lambda l, o, i: (l, 0, o)),
                  pl.BlockSpec((GRP, 2 * r1, r2), lambda l, o, i: (i, 0, 0))],
        out_specs=pl.BlockSpec((None, None, GRP, r1, c), lambda l, o, i: (l, o, i, 0, 0)),
        out_shape=jax.ShapeDtypeStruct((nl, 2, r1, r1, c), jnp.uint32),
        compiler_params=_params("parallel", "parallel", "parallel"),
        name="filt_dft1",
    )(x4, asum, tabs["g_real"])
    return pl.pallas_call(
        functools.partial(_filt_stage2_kernel, inv_n=1.0 / kern.shape[1]),
        grid=(nl, 2, r1 // GRP),
        in_specs=[pl.BlockSpec((None, None, r1, GRP, c), lambda l, o, i: (l, o, 0, i, 0)),
                  pl.BlockSpec((2 * r1, 2 * r1), lambda l, o, i: (0, 0))],
        out_specs=pl.BlockSpec((None, None, 2, GRP, r1, c), lambda l, o, i: (l, o, 0, i, 0, 0)),
        out_shape=jax.ShapeDtypeStruct((nl, 2, 2, r1, r1, c), BF16),
        compiler_params=_params("parallel", "parallel", "parallel"),
        name="filt_dft2",
    )(s1, tabs["m3"])


def _conv_stage1_kernel(x_ref, g_ref, o_ref):
    for j in range(GRP):
        zin = jnp.concatenate([x_ref[0, :, j, :], x_ref[1, :, j, :]], axis=0).astype(BF16)
        o_ref[j] = _pack(jnp.dot(g_ref[j], zin, preferred_element_type=F32))


def _conv_stage2_kernel(b_ref, kf_ref, m3_ref, ginv_ref, o_ref):
    r1 = o_ref.shape[1]
    for j in range(GRP):
        x = jnp.dot(m3_ref[...], _unpack(b_ref[:, j, :]), preferred_element_type=F32)
        xr, xi = x[:r1], x[r1:]
        kr, ki = kf_ref[0, j].astype(F32), kf_ref[1, j].astype(F32)
        y = jnp.concatenate([xr * kr - xi * ki, xr * ki + xi * kr], axis=0).astype(BF16)
        o_ref[j] = _pack(jnp.dot(ginv_ref[j], y, preferred_element_type=F32))


def _conv_stage3_mid_kernel(q_ref, m5_ref, u_ref, x1_ref, d_ref, g_ref, o_ref, s1_ref):
    half = u_ref.shape[1]
    d = d_ref[...]
    for j in range(GRP):
        y = jnp.dot(m5_ref[...], _unpack(q_ref[:, j, :]), preferred_element_type=F32)
        zs = []
        for bi in range(2):
            conv = y[bi * half:(bi + 1) * half] + u_ref[bi, :, j, :] * d
            z = x1_ref[bi, :, j, :] * conv
            o_ref[bi, :, j, :] = z
            zs.append(z)
        zin = jnp.concatenate(zs, axis=0).astype(BF16)
        s1_ref[j] = _pack(jnp.dot(g_ref[j], zin, preferred_element_type=F32))


def _conv_stage3_last_kernel(q_ref, m5_ref, u_ref, x2_ref, gate_ref, d_ref, o_ref):
    half = u_ref.shape[1]
    d = d_ref[...]
    for j in range(GRP):
        y = jnp.dot(m5_ref[...], _unpack(q_ref[:, j, :]), preferred_element_type=F32)
        for bi in range(2):
            g = gate_ref[bi, :, j, :]
            conv = y[bi * half:(bi + 1) * half] + u_ref[bi, :, j, :] * d
            o_ref[bi, :, j, :] = x2_ref[bi, :, j, :] * conv * (g / (1.0 + jnp.exp(-g)))


def _conv_dft1(src4, blk, tabs):
    c = D_HYENA
    r1 = LANES
    b, r2h = src4.shape[0], src4.shape[1]
    return pl.pallas_call(
        _conv_stage1_kernel,
        grid=(r1 // GRP,),
        in_specs=[pl.BlockSpec((b, r2h, GRP, c), lambda i: (0, 0, i, blk)),
                  pl.BlockSpec((GRP, 2 * r1, 2 * r2h), lambda i: (i, 0, 0))],
        out_specs=pl.BlockSpec((GRP, r1, c), lambda i: (i, 0, 0)),
        out_shape=jax.ShapeDtypeStruct((r1, r1, c), jnp.uint32),
        compiler_params=_params("parallel"),
        name="conv_dft1",
    )(src4, tabs["g_fwd"])


def _conv_dft2(s1, spec, layer, order, tabs):
    c = D_HYENA
    r1 = LANES
    return pl.pallas_call(
        _conv_stage2_kernel,
        grid=(r1 // GRP,),
        in_specs=[pl.BlockSpec((r1, GRP, c), lambda i: (0, i, 0)),
                  pl.BlockSpec((None, None, 2, GRP, r1, c), lambda i: (layer, order, 0, i, 0, 0)),
                  pl.BlockSpec((2 * r1, 2 * r1), lambda i: (0, 0)),
                  pl.BlockSpec((GRP, 2 * r1, 2 * r1), lambda i: (i, 0, 0))],
        out_specs=pl.BlockSpec((GRP, r1, c), lambda i: (i, 0, 0)),
        out_shape=jax.ShapeDtypeStruct((r1, r1, c), jnp.uint32),
        compiler_params=_params("parallel"),
        name="conv_dft2",
    )(s1, spec, tabs["m3"], tabs["g_inv"])


def hyena_branch(csc, proj3, spec, layer, filt_bias, tabs, seq):
    c = D_HYENA
    r1 = LANES
    r2h = seq // r1
    b = csc.shape[0]
    steps = r1 // GRP
    csc4 = csc.reshape(b, r2h, r1, 3 * c)
    proj4 = proj3.reshape(b, r2h, r1, D_IN)
    qspec = pl.BlockSpec((r1, GRP, c), lambda i: (0, i, 0))
    m5spec = pl.BlockSpec((2 * r2h, 2 * r1), lambda i: (0, 0))
    dspec = pl.BlockSpec((1, c), lambda i: (0, 0))
    tblk = lambda blk: pl.BlockSpec((b, r2h, GRP, c), lambda i: (0, 0, i, blk))

    q1 = _conv_dft2(_conv_dft1(csc4, 0, tabs), spec, layer, 0, tabs)
    z, s1 = pl.pallas_call(
        _conv_stage3_mid_kernel,
        grid=(steps,),
        in_specs=[qspec, m5spec, tblk(0), tblk(1), dspec,
                  pl.BlockSpec((GRP, 2 * r1, 2 * r2h), lambda i: (i, 0, 0))],
        out_specs=[tblk(0), pl.BlockSpec((GRP, r1, c), lambda i: (i, 0, 0))],
        out_shape=[jax.ShapeDtypeStruct((b, r2h, r1, c), F32),
                   jax.ShapeDtypeStruct((r1, r1, c), jnp.uint32)],
        compiler_params=_params("parallel"),
        name="conv_idft_mid",
    )(q1, tabs["m5"], csc4, csc4, filt_bias[0:1], tabs["g_fwd"])

    q2 = _conv_dft2(s1, spec, layer, 1, tabs)
    y = pl.pallas_call(
        _conv_stage3_last_kernel,
        grid=(steps,),
        in_specs=[qspec, m5spec, tblk(0), tblk(2), tblk(HY_PROJ // c), dspec],
        out_specs=tblk(0),
        out_shape=jax.ShapeDtypeStruct((b, r2h, r1, c), F32),
        compiler_params=_params("parallel"),
        name="conv_idft_last",
    )(q2, tabs["m5"], z, csc4, proj4, filt_bias[1:2])
    return y.reshape(b, seq, c)


def _out_proj_kernel(yh_ref, ya_ref, w_ref, h_ref, g_ref, o_ref, *, final):
    k = yh_ref.shape[1]
    acc = jnp.dot(yh_ref[...].astype(BF16), w_ref[:k, :], preferred_element_type=F32)
    acc = acc + jnp.dot(ya_ref[...].astype(BF16), w_ref[k:, :], preferred_element_type=F32)
    hn = h_ref[...] + acc
    if final:
        ms = jnp.mean(hn * hn, axis=-1, keepdims=True)
        hn = hn * lax.rsqrt(ms + NORM_EPS) * g_ref[...]
    o_ref[...] = hn


def out_proj(yh, ya, w_bf16, h2, g, final, tm=512):
    m, d = h2.shape
    k = yh.shape[1]
    return pl.pallas_call(
        functools.partial(_out_proj_kernel, final=final),
        grid=(m // tm,),
        in_specs=[pl.BlockSpec((tm, k), lambda i: (i, 0)),
                  pl.BlockSpec((tm, k), lambda i: (i, 0)),
                  pl.BlockSpec((2 * k, d), lambda i: (0, 0)),
                  pl.BlockSpec((tm, d), lambda i: (i, 0)),
                  pl.BlockSpec((1, d), lambda i: (0, 0))],
        out_specs=pl.BlockSpec((tm, d), lambda i: (i, 0)),
        out_shape=jax.ShapeDtypeStruct((m, d), F32),
        compiler_params=_params("parallel"),
        name="out_proj",
    )(yh, ya, w_bf16, h2, g.reshape(1, d))


def _rope_tables(seq):
    inv_freq = 1.0 / (ROPE_THETA ** (jnp.arange(0, QK_DIM, 2, dtype=F32) / QK_DIM))
    ang = jnp.arange(seq, dtype=F32)[:, None] * inv_freq[None, :]
    ang = jnp.concatenate([ang, ang, ang, ang], axis=-1)
    sign = jnp.where((jnp.arange(LANES) % QK_DIM) < QK_DIM // 2, -1.0, 1.0).astype(F32)
    return jnp.cos(ang), jnp.sin(ang) * sign[None, :]


def _filter_features(seq):
    t = jnp.linspace(0.0, 1.0, seq, dtype=F32)[:, None]
    bands = (EMB_DIM - 1) // 2
    wpos = 2.0 * math.pi * jnp.arange(seq, dtype=F32) / seq
    fb = jnp.linspace(1e-4, bands - 1, bands, dtype=F32)
    ph = wpos[:, None] * fb[None, :]
    z = jnp.concatenate([t, jnp.cos(ph), -jnp.sin(ph)], axis=-1)
    z2 = jnp.concatenate([z, jnp.roll(z[::-1], 1, axis=0)], axis=0)
    z_pad = jnp.pad(z2, ((0, 0), (0, LANES - EMB_DIM)))
    max_decay = math.log(DECAY_TARGET) / FAST_DECAY_PCT
    min_decay = math.log(DECAY_TARGET) / SLOW_DECAY_PCT
    deltas = jnp.linspace(min_decay, max_decay, D_HYENA, dtype=F32)
    return z_pad, (-jnp.abs(deltas))[None, :]


def kernel(x, norm_g, w_in, conv_w, conv_b, filt_w1, filt_b1, filt_w2, filt_b2, filt_w3, filt_b3,
           filt_freq, filt_w4, filt_bias, lam_q1, lam_k1, lam_q2, lam_k2, subln_g, w_out, final_g):
    b, seq, d = x.shape
    depth = w_in.shape[0]
    m = b * seq
    tabs = {k: jnp.asarray(v).astype(BF16) for k, v in _dft_tables(seq).items()}
    cos_t, sin_t = _rope_tables(seq)
    z_pad, neg_abs_delta = _filter_features(seq)

    kern, asum = filter_taps(z_pad, filt_w1, filt_b1, filt_w2, filt_b2, filt_w3, filt_b3, filt_freq, filt_w4,
                             neg_abs_delta)
    spec = filter_spectrum(kern, asum, tabs)

    h = x.reshape(m, d)
    for layer in range(depth):
        lam_init = 0.8 - 0.6 * math.exp(-0.3 * layer)
        proj = in_proj(h, norm_g[layer], w_in[layer].astype(BF16))
        proj3 = proj.reshape(b, seq, D_IN)

        q0, q1, kr, vt = rope(proj, cos_t, sin_t, b, seq)
        sh = (b, seq, D_ATTN)
        y_a = attention(q0.reshape(sh), q1.reshape(sh), kr.reshape(sh), vt, proj3,
                        lam_q1[layer], lam_k1[layer], lam_q2[layer], lam_k2[layer], subln_g[layer], lam_init)

        csc = short_conv(proj3, conv_w[layer], conv_b[layer])
        y_h = hyena_branch(csc, proj3, spec, layer, filt_bias[layer], tabs, seq)

        h = out_proj(y_h.reshape(m, D_HYENA), y_a.reshape(m, D_ATTN), w_out[layer].astype(BF16), h,
                     final_g, final=(layer == depth - 1))
    return h.reshape(b, seq, d)
```

```python
import functools
import math

import numpy as np
import jax
import jax.numpy as jnp
from jax import lax
from jax.experimental import pallas as pl
from jax.experimental.pallas import tpu as pltpu

F32 = jnp.float32
BF16 = jnp.bfloat16

D_MODEL = 1024
D_HYENA = 512
D_ATTN = 512
N_HEADS = 4
V_DIM = 128
QK_DIM = 64
HY_PROJ = 3 * D_HYENA
D_IN = 4096
EMB_DIM = 33
FILT_ORDER = 64
DECAY_TARGET = 1e-2
FAST_DECAY_PCT = 0.3
SLOW_DECAY_PCT = 1.5
ROPE_THETA = 10000.0
NORM_EPS = 1e-6
SUBLN_EPS = 1e-5

LANES = 128
VMEM_LIMIT = 48 * 1024 * 1024


def _params(*sem):
    return pltpu.CompilerParams(dimension_semantics=sem, vmem_limit_bytes=VMEM_LIMIT)


VT_ROWS = V_DIM + 16
LOG2E = 1.4426950408889634
HY_COLS = HY_PROJ + D_HYENA


def _rope(t, cos, sin, first):
    rot = jnp.where(first, pltpu.roll(t, LANES - QK_DIM // 2, axis=1), pltpu.roll(t, QK_DIM // 2, axis=1))
    return t * cos + rot * sin


def _in_proj_kernel(x_ref, g_ref, w_ref, cos_ref, sin_ref, hy_ref, q0_ref, q1_ref, ko_ref, vt_ref, ag_ref,
                    u_ref):
    j = pl.program_id(1)

    @pl.when(j == 0)
    def _():
        x = x_ref[...]
        ms = jnp.mean(x * x, axis=-1, keepdims=True)
        u_ref[...] = (x * lax.rsqrt(ms + NORM_EPS) * g_ref[...]).astype(BF16)

    acc = jnp.dot(u_ref[...], w_ref[...], preferred_element_type=F32)
    w = N_HEADS * LANES

    @pl.when(j < 2)
    def _():
        hy_ref[...] = acc

    @pl.when(j == 2)
    def _():
        cos = cos_ref[...]
        sin = sin_ref[...]
        lane = lax.broadcasted_iota(jnp.int32, cos.shape, 1)
        first = (lane & (QK_DIM - 1)) < QK_DIM // 2
        map0 = lane < QK_DIM
        for h in range(N_HEADS):
            sl = slice(h * LANES, (h + 1) * LANES)
            r = _rope(acc[:, sl], cos, sin, first) * (QK_DIM ** -0.5 * LOG2E)
            q0_ref[:, sl] = jnp.where(map0, r, 0.0).astype(BF16)
            q1_ref[:, sl] = jnp.where(map0, 0.0, r).astype(BF16)
            ko_ref[:, sl] = _rope(acc[:, w + h * LANES:w + (h + 1) * LANES], cos, sin, first).astype(BF16)

    @pl.when(j == 3)
    def _():
        tk = vt_ref.shape[3]
        for h in range(N_HEADS):
            for c in range(vt_ref.shape[1]):
                vt_ref[h, c, :V_DIM, :] = acc[c * tk:(c + 1) * tk, h * LANES:(h + 1) * LANES].T.astype(BF16)
                vt_ref[h, c, V_DIM:, :] = jnp.ones((VT_ROWS - V_DIM, tk), BF16)
        ag_ref[...] = acc[:, w:]


def in_proj(x2, g, w_bf16, cos_t, sin_t, batch, seq, tm=1024, tk=512):
    m, d = x2.shape
    tn = 2 * N_HEADS * LANES
    assert w_bf16.shape[1] == 4 * tn and HY_COLS == 2 * tn
    w = N_HEADS * LANES
    spb = seq // tm
    half = jax.ShapeDtypeStruct((m, w), BF16)
    hspec = pl.BlockSpec((tm, w), lambda i, j: (i, 0))
    return pl.pallas_call(
        _in_proj_kernel,
        grid=(m // tm, 4),
        in_specs=[pl.BlockSpec((tm, d), lambda i, j: (i, 0)),
                  pl.BlockSpec((1, d), lambda i, j: (0, 0)),
                  pl.BlockSpec((d, tn), lambda i, j: (0, j)),
                  pl.BlockSpec((tm, LANES), lambda i, j: (i % spb, 0)),
                  pl.BlockSpec((tm, LANES), lambda i, j: (i % spb, 0))],
        out_specs=[pl.BlockSpec((tm, tn), lambda i, j: (i, jnp.minimum(j, 1))),
                   hspec, hspec, hspec,
                   pl.BlockSpec((None, N_HEADS, tm // tk, VT_ROWS, tk), lambda i, j: (i // spb, 0, i % spb, 0, 0)),
                   hspec],
        out_shape=[jax.ShapeDtypeStruct((m, HY_COLS), F32), half, half, half,
                   jax.ShapeDtypeStruct((batch, N_HEADS, seq // tk, VT_ROWS, tk), BF16),
                   jax.ShapeDtypeStruct((m, w), F32)],
        scratch_shapes=[pltpu.VMEM((tm, d), BF16)],
        compiler_params=_params("parallel", "arbitrary"),
        name="in_proj",
    )(x2, g.reshape(1, d), w_bf16, cos_t, sin_t)


def _attn_kernel(q0_ref, q1_ref, k_ref, vt_ref, gate_ref, lq1_ref, lk1_ref, lq2_ref, lk2_ref,
                 subg_ref, o_ref, *, lam_init):
    tq = q0_ref.shape[0]
    nk, _, tk = vt_ref.shape
    q0 = q0_ref[...]
    q1 = q1_ref[...]

    def scores(q, kj):
        return lax.dot_general(kj, q, (((1,), (1,)), ((), ())), preferred_element_type=F32)

    def local(st):
        mx = jnp.max(st, axis=0, keepdims=True)
        return mx, jnp.exp2((st - mx).astype(BF16))

    def merge(m, acc, mx, pv):
        m_new = jnp.maximum(m, mx)
        return m_new, jnp.exp2(m - m_new) * acc + jnp.exp2(mx - m_new) * pv

    neg = jnp.full((1, tq), -1e30, F32)
    zacc = jnp.zeros((vt_ref.shape[1], tq), F32)
    m0, a0, m1, a1 = neg, zacc, neg, zacc
    k0 = k_ref[0:tk, :]
    nxt = (scores(q0, k0), scores(q1, k0))
    for j in range(nk):
        st0, st1 = nxt
        if j + 1 < nk:
            kn = k_ref[(j + 1) * tk:(j + 2) * tk, :]
            nxt = (scores(q0, kn), scores(q1, kn))
        vtj = vt_ref[j]
        mx0, p0 = local(st0)
        mx1, p1 = local(st1)
        m0, a0 = merge(m0, a0, mx0, jnp.dot(vtj, p0, preferred_element_type=F32))
        m1, a1 = merge(m1, a1, mx1, jnp.dot(vtj, p1, preferred_element_type=F32))
    l0, l1 = a0[V_DIM:V_DIM + 1], a1[V_DIM:V_DIM + 1]
    a0, a1 = a0[:V_DIM], a1[:V_DIM]

    lam = (jnp.exp(jnp.sum(lq1_ref[...] * lk1_ref[...], axis=-1, keepdims=True))
           - jnp.exp(jnp.sum(lq2_ref[...] * lk2_ref[...], axis=-1, keepdims=True)) + lam_init)
    ot = a0 / l0 - lam * (a1 / l1)
    ot = ot * lax.rsqrt(jnp.mean(ot * ot, axis=0, keepdims=True) + SUBLN_EPS)
    o = ot.T * subg_ref[...] * (1.0 - lam_init)
    g = gate_ref[...]
    o_ref[...] = (o * (g / (1.0 + jnp.exp(-g)))).astype(o_ref.dtype)


def attention(q0, q1, k, vt, gate, lq1, lk1, lq2, lk2, subg, lam_init, tq=512):
    b, s, _ = q0.shape
    nk, tk = vt.shape[2], vt.shape[4]
    qspec = pl.BlockSpec((None, tq, LANES), lambda bi, h, qi: (bi, qi, h))
    small = pl.BlockSpec((1, QK_DIM), lambda bi, h, qi: (0, 0))
    return pl.pallas_call(
        functools.partial(_attn_kernel, lam_init=lam_init),
        grid=(b, N_HEADS, s // tq),
        in_specs=[qspec, qspec,
                  pl.BlockSpec((None, s, LANES), lambda bi, h, qi: (bi, 0, h)),
                  pl.BlockSpec((None, None, nk, VT_ROWS, tk), lambda bi, h, qi: (bi, h, 0, 0, 0)),
                  qspec,
                  small, small, small, small,
                  pl.BlockSpec((1, V_DIM), lambda bi, h, qi: (0, 0))],
        out_specs=pl.BlockSpec((None, tq, LANES), lambda bi, h, qi: (bi, qi, h)),
        out_shape=jax.ShapeDtypeStruct((b, s, D_ATTN), BF16),
        compiler_params=_params("parallel", "parallel", "arbitrary"),
        name="diff_attn",
    )(q0, q1, k, vt, gate, lq1.reshape(1, -1), lk1.reshape(1, -1), lq2.reshape(1, -1),
      lk2.reshape(1, -1), subg.reshape(1, -1))


def _short_conv_kernel(u_ref, w_ref, b_ref, o_ref):
    u = u_ref[...]
    n = u.shape[0]
    row = lax.broadcasted_iota(jnp.int32, u.shape, 0)
    prev = jnp.where(row == 0, 0.0, pltpu.roll(u, 1, axis=0))
    nxt = jnp.where(row == n - 1, 0.0, pltpu.roll(u, n - 1, axis=0))
    o_ref[...] = prev * w_ref[0:1, :] + u * w_ref[1:2, :] + nxt * w_ref[2:3, :] + b_ref[...]


def short_conv(hy3, conv_w, conv_b):
    b, s, _ = hy3.shape
    nblk = HY_PROJ // LANES
    return pl.pallas_call(
        _short_conv_kernel,
        grid=(b, nblk),
        in_specs=[pl.BlockSpec((None, s, LANES), lambda bi, j: (bi, 0, j)),
                  pl.BlockSpec((3, LANES), lambda bi, j: (0, j)),
                  pl.BlockSpec((1, LANES), lambda bi, j: (0, j))],
        out_specs=pl.BlockSpec((None, s, LANES), lambda bi, j: (bi, 0, j)),
        out_shape=jax.ShapeDtypeStruct((b, s, HY_PROJ), F32),
        compiler_params=_params("parallel", "parallel"),
        name="short_conv",
    )(hy3, conv_w, conv_b.reshape(1, -1))


def _filter_kernel(z_ref, w1_ref, b1_ref, w2_ref, b2_ref, w3_ref, b3_ref, fr_ref, w4_ref, nd_ref,
                   k_ref, asum_ref, *, half_steps):
    i = pl.program_id(0)
    hi = lax.Precision.HIGHEST
    z = z_ref[...]
    fr = fr_ref[...]
    h = jnp.sin(fr * (jnp.dot(z, w1_ref[...], precision=hi, preferred_element_type=F32) + b1_ref[...]))
    h = jnp.sin(fr * (jnp.dot(h, w2_ref[...], precision=hi, preferred_element_type=F32) + b2_ref[...]))
    h = jnp.sin(fr * (jnp.dot(h, w3_ref[...], precision=hi, preferred_element_type=F32) + b3_ref[...]))
    hf = jnp.dot(h.astype(BF16), w4_ref[...].astype(BF16), preferred_element_type=F32)
    decay = jnp.exp(z[:, 0:1] * nd_ref[...])
    tt = z.shape[0]
    row = lax.broadcasted_iota(jnp.int32, (tt, D_HYENA), 0)
    centre = jnp.logical_and(row == 0, i == half_steps)

    @pl.when(i == 0)
    def _():
        asum_ref[...] = jnp.zeros_like(asum_ref)

    for layer in range(2):
        for o in range(2):
            sl = slice(o * D_HYENA, (o + 1) * D_HYENA)
            src = slice((2 * layer + o) * D_HYENA, (2 * layer + o + 1) * D_HYENA)
            kg = jnp.where(centre, 0.0, hf[:, src] * decay)
            k_ref[layer, :, sl] = kg
            asum_ref[layer, :, sl] += jnp.sum(jnp.abs(kg), axis=0, keepdims=True)


def filter_taps(z2_pad, w1, b1, w2, b2, w3, b3, fr, w4, neg_abs_delta, tt=512):
    assert w1.shape[0] == 2, "the filter MLP packs exactly two layers into the 128 lanes"
    n = z2_pad.shape[0]
    r = FILT_ORDER
    c2 = 2 * D_HYENA
    steps = n // tt
    zero = jnp.zeros((r, r), F32)
    blockdiag = lambda w: jnp.concatenate([jnp.concatenate([w[0], zero], axis=1),
                                           jnp.concatenate([zero, w[1]], axis=1)], axis=0)
    cat = lambda v: jnp.concatenate([v[0], v[1]], axis=-1).reshape(1, 2 * r)
    w1c = jnp.pad(jnp.concatenate([w1[0], w1[1]], axis=1), ((0, LANES - EMB_DIM), (0, 0)))
    zc = jnp.zeros((r, c2), F32)
    w4c = jnp.stack([jnp.concatenate([jnp.concatenate([w4[0][:, hf * c2:(hf + 1) * c2], zc], axis=1),
                                      jnp.concatenate([zc, w4[1][:, hf * c2:(hf + 1) * c2]], axis=1)], axis=0)
                     for hf in range(2)])
    full = lambda shp: pl.BlockSpec(shp, lambda i: (0, 0))
    return pl.pallas_call(
        functools.partial(_filter_kernel, half_steps=steps // 2),
        grid=(steps,),
        in_specs=[pl.BlockSpec((tt, LANES), lambda i: (i, 0)),
                  full((LANES, 2 * r)), full((1, 2 * r)), full((2 * r, 2 * r)), full((1, 2 * r)),
                  full((2 * r, 2 * r)), full((1, 2 * r)), full((1, 2 * r)),
                  pl.BlockSpec((None, 2 * r, 2 * c2), lambda i: (i // (steps // 2), 0, 0)),
                  full((1, D_HYENA))],
        out_specs=[pl.BlockSpec((2, tt, c2), lambda i: (0, i, 0)),
                   pl.BlockSpec((2, 1, c2), lambda i: (0, 0, 0))],
        out_shape=[jax.ShapeDtypeStruct((2, n, c2), F32),
                   jax.ShapeDtypeStruct((2, 1, c2), F32)],
        compiler_params=_params("arbitrary"),
        name="hyena_filter",
    )(z2_pad, w1c, cat(b1), blockdiag(w2), cat(b2), blockdiag(w3), cat(b3), cat(fr), w4c, neg_abs_delta)


def _stack_c(m):
    return np.block([[m.real, -m.imag], [m.imag, m.real]])


def _interleave(a, axis):
    h = a.shape[axis] // 2
    idx = np.arange(2 * h).reshape(2, h).T.reshape(-1)
    return np.take(a, idx, axis=axis)


@functools.lru_cache(maxsize=None)
def _dft_tables(seq):
    n = 2 * seq
    r1 = LANES
    r2h = seq // r1
    n1 = np.arange(r1)
    k2 = np.arange(r1)
    n2 = np.arange(r2h)
    e = (k2[None, :, None] * (n1[:, None, None] + r1 * n2[None, None, :])) % n
    t = np.exp(-2j * np.pi * e / n)
    g_fwd = np.concatenate([np.concatenate([t.real, -t.imag], axis=2),
                            np.concatenate([t.imag, t.real], axis=2)], axis=1)
    g_fwd = _interleave(g_fwd, 1)
    n2f = np.arange(2 * r2h)
    e = (k2[None, :, None] * (n1[:, None, None] + r1 * n2f[None, None, :])) % n
    tf = np.exp(-2j * np.pi * e / n)
    g_real = _interleave(np.concatenate([tf.real, tf.imag], axis=1), 1)
    f = np.exp(-2j * np.pi * ((n1[:, None] * n1[None, :]) % r1) / r1)
    m3 = _interleave(_stack_c(f), 1)
    e = (n1[None, :, None] * (r1 * n1[None, None, :] + k2[:, None, None])) % n
    einv = np.exp(2j * np.pi * e / n)
    g_inv = np.concatenate([np.concatenate([einv.real, -einv.imag], axis=2),
                            np.concatenate([einv.imag, einv.real], axis=2)], axis=1)
    g_inv = _interleave(g_inv, 1)
    d = np.exp(2j * np.pi * ((n2[:, None] * k2[None, :]) % r1) / r1)
    m5 = _interleave(_stack_c(d), 1)
    tabs = dict(g_fwd=g_fwd, g_real=g_real, m3=m3, g_inv=g_inv, m5=m5)
    return {k: v.astype(np.float32) for k, v in tabs.items()}


def _pack(y):
    return pltpu.bitcast(y.astype(BF16), jnp.uint32)


def _unpack(w):
    return pltpu.bitcast(w, BF16)


GRP = 8


def _filt_stage1_kernel(x_ref, s_ref, g_ref, o_ref):
    scale = 1.0 / s_ref[...]
    half = x_ref.shape[0] // 2
    for j in range(GRP):
        x = jnp.concatenate([x_ref[:half, j, :], x_ref[half:, j, :]], axis=0)
        y = jnp.dot(g_ref[j], (x * scale).astype(BF16), preferred_element_type=F32)
        o_ref[j] = _pack(y)


def _filt_stage2_kernel(b_ref, m_ref, o_ref, *, inv_n):
    r1 = o_ref.shape[2]
    for j in range(GRP):
        x = jnp.dot(m_ref[...], _unpack(b_ref[:, j, :]), preferred_element_type=F32) * inv_n
        o_ref[0, j] = x[:r1].astype(o_ref.dtype)
        o_ref[1, j] = x[r1:].astype(o_ref.dtype)


def filter_spectrum(kern, asum, tabs):
    c = D_HYENA
    r1 = LANES
    nl = kern.shape[0]
    r2 = kern.shape[1] // r1
    x4 = kern.reshape(nl, r2, r1, 2 * c)
    s1 = pl.pallas_call(
        _filt_stage1_kernel,
        grid=(nl, 2, r1 // GRP),
        in_specs=[pl.BlockSpec((None, r2, GRP, c), lambda l, o, i: (l, 0, i, o)),
                  pl.BlockSpec((None, 1, c), lambda l, o, i: (l, 0, o)),
                  pl.BlockSpec((GRP, 2 * r1, r2), lambda l, o, i: (i, 0, 0))],
        out_specs=pl.BlockSpec((None, None, GRP, r1, c), lambda l, o, i: (l, o, i, 0, 0)),
        out_shape=jax.ShapeDtypeStruct((nl, 2, r1, r1, c), jnp.uint32),
        compiler_params=_params("parallel", "parallel", "parallel"),
        name="filt_dft1",
    )(x4, asum, tabs["g_real"])
    return pl.pallas_call(
        functools.partial(_filt_stage2_kernel, inv_n=1.0 / kern.shape[1]),
        grid=(nl, 2, r1 // GRP),
        in_specs=[pl.BlockSpec((None, None, r1, GRP, c), lambda l, o, i: (l, o, 0, i, 0)),
                  pl.BlockSpec((2 * r1, 2 * r1), lambda l, o, i: (0, 0))],
        out_specs=pl.BlockSpec((None, None, 2, GRP, r1, c), lambda l, o, i: (l, o, 0, i, 0, 0)),
        out_shape=jax.ShapeDtypeStruct((nl, 2, 2, r1, r1, c), BF16),
        compiler_params=_params("parallel", "parallel", "parallel"),
        name="filt_dft2",
    )(s1, tabs["m3"])


def _conv_stage1_kernel(x_ref, g_ref, o_ref):
    for j in range(GRP):
        zin = jnp.concatenate([x_ref[0, :, j, :], x_ref[1, :, j, :]], axis=0).astype(BF16)
        o_ref[j] = _pack(jnp.dot(g_ref[j], zin, preferred_element_type=F32))


def _conv_stage2_kernel(b_ref, kf_ref, m3_ref, ginv_ref, o_ref):
    r1 = o_ref.shape[1]
    for j in range(GRP):
        x = jnp.dot(m3_ref[...], _unpack(b_ref[:, j, :]), preferred_element_type=F32)
        xr, xi = x[:r1], x[r1:]
        kr, ki = kf_ref[0, j].astype(F32), kf_ref[1, j].astype(F32)
        y = jnp.concatenate([xr * kr - xi * ki, xr * ki + xi * kr], axis=0).astype(BF16)
        o_ref[j] = _pack(jnp.dot(ginv_ref[j], y, preferred_element_type=F32))


def _conv_stage3_mid_kernel(q_ref, m5_ref, u_ref, x1_ref, d_ref, g_ref, o_ref, s1_ref):
    half = u_ref.shape[1]
    d = d_ref[...]
    for j in range(GRP):
        y = jnp.dot(m5_ref[...], _unpack(q_ref[:, j, :]), preferred_element_type=F32)
        zs = []
        for bi in range(2):
            conv = y[bi * half:(bi + 1) * half] + u_ref[bi, :, j, :] * d
            z = x1_ref[bi, :, j, :] * conv
            o_ref[bi, :, j, :] = z
            zs.append(z)
        zin = jnp.concatenate(zs, axis=0).astype(BF16)
        s1_ref[j] = _pack(jnp.dot(g_ref[j], zin, preferred_element_type=F32))


def _conv_stage3_last_kernel(q_ref, m5_ref, u_ref, x2_ref, gate_ref, d_ref, o_ref):
    half = u_ref.shape[1]
    d = d_ref[...]
    for j in range(GRP):
        y = jnp.dot(m5_ref[...], _unpack(q_ref[:, j, :]), preferred_element_type=F32)
        for bi in range(2):
            g = gate_ref[bi, :, j, :]
            conv = y[bi * half:(bi + 1) * half] + u_ref[bi, :, j, :] * d
            o_ref[bi, :, j, :] = x2_ref[bi, :, j, :] * conv * (g / (1.0 + jnp.exp(-g)))


def _conv_dft1(src4, blk, tabs):
    c = D_HYENA
    r1 = LANES
    b, r2h = src4.shape[0], src4.shape[1]
    return pl.pallas_call(
        _conv_stage1_kernel,
        grid=(r1 // GRP,),
        in_specs=[pl.BlockSpec((b, r2h, GRP, c), lambda i: (0, 0, i, blk)),
                  pl.BlockSpec((GRP, 2 * r1, 2 * r2h), lambda i: (i, 0, 0))],
        out_specs=pl.BlockSpec((GRP, r1, c), lambda i: (i, 0, 0)),
        out_shape=jax.ShapeDtypeStruct((r1, r1, c), jnp.uint32),
        compiler_params=_params("parallel"),
        name="conv_dft1",
    )(src4, tabs["g_fwd"])


def _conv_dft2(s1, spec, layer, order, tabs):
    c = D_HYENA
    r1 = LANES
    return pl.pallas_call(
        _conv_stage2_kernel,
        grid=(r1 // GRP,),
        in_specs=[pl.BlockSpec((r1, GRP, c), lambda i: (0, i, 0)),
                  pl.BlockSpec((None, None, 2, GRP, r1, c), lambda i: (layer, order, 0, i, 0, 0)),
                  pl.BlockSpec((2 * r1, 2 * r1), lambda i: (0, 0)),
                  pl.BlockSpec((GRP, 2 * r1, 2 * r1), lambda i: (i, 0, 0))],
        out_specs=pl.BlockSpec((GRP, r1, c), lambda i: (i, 0, 0)),
        out_shape=jax.ShapeDtypeStruct((r1, r1, c), jnp.uint32),
        compiler_params=_params("parallel"),
        name="conv_dft2",
    )(s1, spec, tabs["m3"], tabs["g_inv"])


def hyena_branch(csc, hy3, spec, layer, filt_bias, tabs, seq):
    c = D_HYENA
    r1 = LANES
    r2h = seq // r1
    b = csc.shape[0]
    steps = r1 // GRP
    csc4 = csc.reshape(b, r2h, r1, 3 * c)
    hy4 = hy3.reshape(b, r2h, r1, HY_COLS)
    qspec = pl.BlockSpec((r1, GRP, c), lambda i: (0, i, 0))
    m5spec = pl.BlockSpec((2 * r2h, 2 * r1), lambda i: (0, 0))
    dspec = pl.BlockSpec((1, c), lambda i: (0, 0))
    tblk = lambda blk: pl.BlockSpec((b, r2h, GRP, c), lambda i: (0, 0, i, blk))

    q1 = _conv_dft2(_conv_dft1(csc4, 0, tabs), spec, layer, 0, tabs)
    z, s1 = pl.pallas_call(
        _conv_stage3_mid_kernel,
        grid=(steps,),
        in_specs=[qspec, m5spec, tblk(0), tblk(1), dspec,
                  pl.BlockSpec((GRP, 2 * r1, 2 * r2h), lambda i: (i, 0, 0))],
        out_specs=[tblk(0), pl.BlockSpec((GRP, r1, c), lambda i: (i, 0, 0))],
        out_shape=[jax.ShapeDtypeStruct((b, r2h, r1, c), F32),
                   jax.ShapeDtypeStruct((r1, r1, c), jnp.uint32)],
        compiler_params=_params("parallel"),
        name="conv_idft_mid",
    )(q1, tabs["m5"], csc4, csc4, filt_bias[0:1], tabs["g_fwd"])

    q2 = _conv_dft2(s1, spec, layer, 1, tabs)
    y = pl.pallas_call(
        _conv_stage3_last_kernel,
        grid=(steps,),
        in_specs=[qspec, m5spec, tblk(0), tblk(2), tblk(HY_PROJ // c), dspec],
        out_specs=tblk(0),
        out_shape=jax.ShapeDtypeStruct((b, r2h, r1, c), F32),
        compiler_params=_params("parallel"),
        name="conv_idft_last",
    )(q2, tabs["m5"], z, csc4, hy4, filt_bias[1:2])
    return y.reshape(b, seq, c)


def _out_proj_kernel(yh_ref, ya_ref, w_ref, h_ref, g_ref, o_ref, *, final):
    k = yh_ref.shape[1]
    acc = jnp.dot(yh_ref[...].astype(BF16), w_ref[:k, :], preferred_element_type=F32)
    acc = acc + jnp.dot(ya_ref[...].astype(BF16), w_ref[k:, :], preferred_element_type=F32)
    hn = h_ref[...] + acc
    if final:
        ms = jnp.mean(hn * hn, axis=-1, keepdims=True)
        hn = hn * lax.rsqrt(ms + NORM_EPS) * g_ref[...]
    o_ref[...] = hn


def out_proj(yh, ya, w_bf16, h2, g, final, tm=512):
    m, d = h2.shape
    k = yh.shape[1]
    return pl.pallas_call(
        functools.partial(_out_proj_kernel, final=final),
        grid=(m // tm,),
        in_specs=[pl.BlockSpec((tm, k), lambda i: (i, 0)),
                  pl.BlockSpec((tm, k), lambda i: (i, 0)),
                  pl.BlockSpec((2 * k, d), lambda i: (0, 0)),
                  pl.BlockSpec((tm, d), lambda i: (i, 0)),
                  pl.BlockSpec((1, d), lambda i: (0, 0))],
        out_specs=pl.BlockSpec((tm, d), lambda i: (i, 0)),
        out_shape=jax.ShapeDtypeStruct((m, d), F32),
        compiler_params=_params("parallel"),
        name="out_proj",
    )(yh, ya, w_bf16, h2, g.reshape(1, d))


def _rope_tables(seq):
    inv_freq = 1.0 / (ROPE_THETA ** (jnp.arange(0, QK_DIM, 2, dtype=F32) / QK_DIM))
    ang = jnp.arange(seq, dtype=F32)[:, None] * inv_freq[None, :]
    ang = jnp.concatenate([ang, ang, ang, ang], axis=-1)
    sign = jnp.where((jnp.arange(LANES) % QK_DIM) < QK_DIM // 2, -1.0, 1.0).astype(F32)
    return jnp.cos(ang), jnp.sin(ang) * sign[None, :]


def _filter_features(seq):
    t = jnp.linspace(0.0, 1.0, seq, dtype=F32)[:, None]
    bands = (EMB_DIM - 1) // 2
    wpos = 2.0 * math.pi * jnp.arange(seq, dtype=F32) / seq
    fb = jnp.linspace(1e-4, bands - 1, bands, dtype=F32)
    ph = wpos[:, None] * fb[None, :]
    z = jnp.concatenate([t, jnp.cos(ph), -jnp.sin(ph)], axis=-1)
    z2 = jnp.concatenate([z, jnp.roll(z[::-1], 1, axis=0)], axis=0)
    z_pad = jnp.pad(z2, ((0, 0), (0, LANES - EMB_DIM)))
    max_decay = math.log(DECAY_TARGET) / FAST_DECAY_PCT
    min_decay = math.log(DECAY_TARGET) / SLOW_DECAY_PCT
    deltas = jnp.linspace(min_decay, max_decay, D_HYENA, dtype=F32)
    return z_pad, (-jnp.abs(deltas))[None, :]


def kernel(x, norm_g, w_in, conv_w, conv_b, filt_w1, filt_b1, filt_w2, filt_b2, filt_w3, filt_b3,
           filt_freq, filt_w4, filt_bias, lam_q1, lam_k1, lam_q2, lam_k2, subln_g, w_out, final_g):
    b, seq, d = x.shape
    depth = w_in.shape[0]
    m = b * seq
    tabs = {k: jnp.asarray(v).astype(BF16) for k, v in _dft_tables(seq).items()}
    cos_t, sin_t = _rope_tables(seq)
    z_pad, neg_abs_delta = _filter_features(seq)

    kern, asum = filter_taps(z_pad, filt_w1, filt_b1, filt_w2, filt_b2, filt_w3, filt_b3, filt_freq, filt_w4,
                             neg_abs_delta)
    spec = filter_spectrum(kern, asum, tabs)

    h = x.reshape(m, d)
    for layer in range(depth):
        lam_init = 0.8 - 0.6 * math.exp(-0.3 * layer)
        hy, q0, q1, kr, vt, ag = in_proj(h, norm_g[layer], w_in[layer].astype(BF16), cos_t, sin_t, b, seq)
        hy3 = hy.reshape(b, seq, HY_COLS)
        sh = (b, seq, D_ATTN)
        y_a = attention(q0.reshape(sh), q1.reshape(sh), kr.reshape(sh), vt, ag.reshape(sh),
                        lam_q1[layer], lam_k1[layer], lam_q2[layer], lam_k2[layer], subln_g[layer], lam_init)

        csc = short_conv(hy3, conv_w[layer], conv_b[layer])
        y_h = hyena_branch(csc, hy3, spec, layer, filt_bias[layer], tabs, seq)

        h = out_proj(y_h.reshape(m, D_HYENA), y_a.reshape(m, D_ATTN), w_out[layer].astype(BF16), h,
                     final_g, final=(layer == depth - 1))
    return h.reshape(b, seq, d)
```

```python
import functools
import math

import numpy as np
import jax
import jax.numpy as jnp
from jax import lax
from jax.experimental import pallas as pl
from jax.experimental.pallas import tpu as pltpu

F32 = jnp.float32
BF16 = jnp.bfloat16

D_MODEL = 1024
D_HYENA = 512
D_ATTN = 512
N_HEADS = 4
V_DIM = 128
QK_DIM = 64
HY_PROJ = 3 * D_HYENA
D_IN = 4096
EMB_DIM = 33
FILT_ORDER = 64
DECAY_TARGET = 1e-2
FAST_DECAY_PCT = 0.3
SLOW_DECAY_PCT = 1.5
ROPE_THETA = 10000.0
NORM_EPS = 1e-6
SUBLN_EPS = 1e-5

LANES = 128
VMEM_LIMIT = 48 * 1024 * 1024


def _params(*sem):
    return pltpu.CompilerParams(dimension_semantics=sem, vmem_limit_bytes=VMEM_LIMIT)


VT_ROWS = V_DIM + 16
LOG2E = 1.4426950408889634
HY_COLS = HY_PROJ + D_HYENA


def _rope(t, cos, sin, first):
    rot = jnp.where(first, pltpu.roll(t, LANES - QK_DIM // 2, axis=1), pltpu.roll(t, QK_DIM // 2, axis=1))
    return t * cos + rot * sin


def _in_proj_kernel(x_ref, g_ref, w_ref, cos_ref, sin_ref, hy_ref, q0_ref, q1_ref, ko_ref, vt_ref, ag_ref,
                    u_ref):
    j = pl.program_id(1)

    @pl.when(j == 0)
    def _():
        x = x_ref[...]
        ms = jnp.mean(x * x, axis=-1, keepdims=True)
        u_ref[...] = (x * lax.rsqrt(ms + NORM_EPS) * g_ref[...]).astype(BF16)

    acc = jnp.dot(u_ref[...], w_ref[...], preferred_element_type=F32)
    w = N_HEADS * LANES

    @pl.when(j < 2)
    def _():
        hy_ref[...] = acc

    @pl.when(j == 2)
    def _():
        cos = cos_ref[...]
        sin = sin_ref[...]
        lane = lax.broadcasted_iota(jnp.int32, cos.shape, 1)
        first = (lane & (QK_DIM - 1)) < QK_DIM // 2
        map0 = lane < QK_DIM
        for h in range(N_HEADS):
            sl = slice(h * LANES, (h + 1) * LANES)
            r = _rope(acc[:, sl], cos, sin, first) * (QK_DIM ** -0.5 * LOG2E)
            q0_ref[:, sl] = jnp.where(map0, r, 0.0).astype(BF16)
            q1_ref[:, sl] = jnp.where(map0, 0.0, r).astype(BF16)
            ko_ref[:, sl] = _rope(acc[:, w + h * LANES:w + (h + 1) * LANES], cos, sin, first).astype(BF16)

    @pl.when(j == 3)
    def _():
        tk = vt_ref.shape[3]
        for h in range(N_HEADS):
            for c in range(vt_ref.shape[1]):
                vt_ref[h, c, :V_DIM, :] = acc[c * tk:(c + 1) * tk, h * LANES:(h + 1) * LANES].T.astype(BF16)
                vt_ref[h, c, V_DIM:, :] = jnp.ones((VT_ROWS - V_DIM, tk), BF16)
        ag_ref[...] = acc[:, w:]


def in_proj(x2, g, w_bf16, cos_t, sin_t, batch, seq, tm=1024, tk=512):
    m, d = x2.shape
    tn = 2 * N_HEADS * LANES
    assert w_bf16.shape[1] == 4 * tn and HY_COLS == 2 * tn
    w = N_HEADS * LANES
    spb = seq // tm
    half = jax.ShapeDtypeStruct((m, w), BF16)
    hspec = pl.BlockSpec((tm, w), lambda i, j: (i, 0))
    return pl.pallas_call(
        _in_proj_kernel,
        grid=(m // tm, 4),
        in_specs=[pl.BlockSpec((tm, d), lambda i, j: (i, 0)),
                  pl.BlockSpec((1, d), lambda i, j: (0, 0)),
                  pl.BlockSpec((d, tn), lambda i, j: (0, j)),
                  pl.BlockSpec((tm, LANES), lambda i, j: (i % spb, 0)),
                  pl.BlockSpec((tm, LANES), lambda i, j: (i % spb, 0))],
        out_specs=[pl.BlockSpec((tm, tn), lambda i, j: (i, jnp.minimum(j, 1))),
                   hspec, hspec, hspec,
                   pl.BlockSpec((None, N_HEADS, tm // tk, VT_ROWS, tk), lambda i, j: (i // spb, 0, i % spb, 0, 0)),
                   hspec],
        out_shape=[jax.ShapeDtypeStruct((m, HY_COLS), F32), half, half, half,
                   jax.ShapeDtypeStruct((batch, N_HEADS, seq // tk, VT_ROWS, tk), BF16),
                   jax.ShapeDtypeStruct((m, w), F32)],
        scratch_shapes=[pltpu.VMEM((tm, d), BF16)],
        compiler_params=_params("parallel", "arbitrary"),
        name="in_proj",
    )(x2, g.reshape(1, d), w_bf16, cos_t, sin_t)


def _attn_kernel(q0_ref, q1_ref, k_ref, vt_ref, gate_ref, lq1_ref, lk1_ref, lq2_ref, lk2_ref,
                 subg_ref, o_ref, s_scr, *, lam_init):
    tq = q0_ref.shape[0]
    nk, _, tk = vt_ref.shape
    q0 = q0_ref[...]
    q1 = q1_ref[...]

    def produce(j, slot):
        kj = k_ref[pl.ds(pl.multiple_of(j * tk, tk), tk), :]
        mx = []
        for c, q in enumerate((q0, q1)):
            st = lax.dot_general(kj, q, (((1,), (1,)), ((), ())), preferred_element_type=F32)
            s_scr[slot, c] = st
            mx.append(jnp.max(st, axis=0, keepdims=True))
        return mx

    def consume(j, slot, mx, state):
        vtj = vt_ref[j]
        out = []
        for c in range(2):
            m, acc = state[2 * c], state[2 * c + 1]
            p = jnp.exp2((s_scr[slot, c] - mx[c]).astype(BF16))
            pv = jnp.dot(vtj, p, preferred_element_type=F32)
            m_new = jnp.maximum(m, mx[c])
            out += [m_new, jnp.exp2(m - m_new) * acc + jnp.exp2(mx[c] - m_new) * pv]
        return tuple(out)

    neg = jnp.full((1, tq), -1e30, F32)
    zacc = jnp.zeros((vt_ref.shape[1], tq), F32)

    def body(i, carry):
        mxa, state = carry[:2], carry[2:]
        j = 2 * i
        mxb = produce(j + 1, 1)
        state = consume(j, 0, mxa, state)
        mxa = produce(jnp.minimum(j + 2, nk - 1), 0)
        state = consume(j + 1, 1, mxb, state)
        return tuple(mxa) + tuple(state)

    out = lax.fori_loop(0, nk // 2, body, tuple(produce(0, 0)) + (neg, zacc, neg, zacc))
    a0, a1 = out[3], out[5]
    l0, l1 = a0[V_DIM:V_DIM + 1], a1[V_DIM:V_DIM + 1]
    a0, a1 = a0[:V_DIM], a1[:V_DIM]

    lam = (jnp.exp(jnp.sum(lq1_ref[...] * lk1_ref[...], axis=-1, keepdims=True))
           - jnp.exp(jnp.sum(lq2_ref[...] * lk2_ref[...], axis=-1, keepdims=True)) + lam_init)
    ot = a0 / l0 - lam * (a1 / l1)
    ot = ot * lax.rsqrt(jnp.mean(ot * ot, axis=0, keepdims=True) + SUBLN_EPS)
    o = ot.T * subg_ref[...] * (1.0 - lam_init)
    g = gate_ref[...]
    o_ref[...] = (o * (g / (1.0 + jnp.exp(-g)))).astype(o_ref.dtype)


def attention(q0, q1, k, vt, gate, lq1, lk1, lq2, lk2, subg, lam_init, tq=512):
    b, s, _ = q0.shape
    nk, tk = vt.shape[2], vt.shape[4]
    qspec = pl.BlockSpec((None, tq, LANES), lambda bi, h, qi: (bi, qi, h))
    small = pl.BlockSpec((1, QK_DIM), lambda bi, h, qi: (0, 0))
    return pl.pallas_call(
        functools.partial(_attn_kernel, lam_init=lam_init),
        grid=(b, N_HEADS, s // tq),
        in_specs=[qspec, qspec,
                  pl.BlockSpec((None, s, LANES), lambda bi, h, qi: (bi, 0, h)),
                  pl.BlockSpec((None, None, nk, VT_ROWS, tk), lambda bi, h, qi: (bi, h, 0, 0, 0)),
                  qspec,
                  small, small, small, small,
                  pl.BlockSpec((1, V_DIM), lambda bi, h, qi: (0, 0))],
        out_specs=pl.BlockSpec((None, tq, LANES), lambda bi, h, qi: (bi, qi, h)),
        out_shape=jax.ShapeDtypeStruct((b, s, D_ATTN), BF16),
        scratch_shapes=[pltpu.VMEM((2, 2, tk, tq), F32)],
        compiler_params=_params("parallel", "parallel", "arbitrary"),
        name="diff_attn",
    )(q0, q1, k, vt, gate, lq1.reshape(1, -1), lk1.reshape(1, -1), lq2.reshape(1, -1),
      lk2.reshape(1, -1), subg.reshape(1, -1))


def _short_conv_kernel(u_ref, w_ref, b_ref, o_ref):
    u = u_ref[...]
    n = u.shape[0]
    row = lax.broadcasted_iota(jnp.int32, u.shape, 0)
    prev = jnp.where(row == 0, 0.0, pltpu.roll(u, 1, axis=0))
    nxt = jnp.where(row == n - 1, 0.0, pltpu.roll(u, n - 1, axis=0))
    o_ref[...] = prev * w_ref[0:1, :] + u * w_ref[1:2, :] + nxt * w_ref[2:3, :] + b_ref[...]


def short_conv(hy3, conv_w, conv_b):
    b, s, _ = hy3.shape
    nblk = HY_PROJ // LANES
    return pl.pallas_call(
        _short_conv_kernel,
        grid=(b, nblk),
        in_specs=[pl.BlockSpec((None, s, LANES), lambda bi, j: (bi, 0, j)),
                  pl.BlockSpec((3, LANES), lambda bi, j: (0, j)),
                  pl.BlockSpec((1, LANES), lambda bi, j: (0, j))],
        out_specs=pl.BlockSpec((None, s, LANES), lambda bi, j: (bi, 0, j)),
        out_shape=jax.ShapeDtypeStruct((b, s, HY_PROJ), F32),
        compiler_params=_params("parallel", "parallel"),
        name="short_conv",
    )(hy3, conv_w, conv_b.reshape(1, -1))


def _filter_kernel(z_ref, w1_ref, b1_ref, w2_ref, b2_ref, w3_ref, b3_ref, fr_ref, w4_ref, nd_ref,
                   k_ref, asum_ref, *, half_steps):
    i = pl.program_id(0)
    hi = lax.Precision.HIGHEST
    z = z_ref[...]
    fr = fr_ref[...]
    h = jnp.sin(fr * (jnp.dot(z, w1_ref[...], precision=hi, preferred_element_type=F32) + b1_ref[...]))
    h = jnp.sin(fr * (jnp.dot(h, w2_ref[...], precision=hi, preferred_element_type=F32) + b2_ref[...]))
    h = jnp.sin(fr * (jnp.dot(h, w3_ref[...], precision=hi, preferred_element_type=F32) + b3_ref[...]))
    hf = jnp.dot(h.astype(BF16), w4_ref[...].astype(BF16), preferred_element_type=F32)
    decay = jnp.exp(z[:, 0:1] * nd_ref[...])
    tt = z.shape[0]
    row = lax.broadcasted_iota(jnp.int32, (tt, D_HYENA), 0)
    centre = jnp.logical_and(row == 0, i == half_steps)

    @pl.when(i == 0)
    def _():
        asum_ref[...] = jnp.zeros_like(asum_ref)

    for layer in range(2):
        for o in range(2):
            sl = slice(o * D_HYENA, (o + 1) * D_HYENA)
            src = slice((2 * layer + o) * D_HYENA, (2 * layer + o + 1) * D_HYENA)
            kg = jnp.where(centre, 0.0, hf[:, src] * decay)
            k_ref[layer, :, sl] = kg
            asum_ref[layer, :, sl] += jnp.sum(jnp.abs(kg), axis=0, keepdims=True)


def filter_taps(z2_pad, w1, b1, w2, b2, w3, b3, fr, w4, neg_abs_delta, tt=512):
    assert w1.shape[0] == 2, "the filter MLP packs exactly two layers into the 128 lanes"
    n = z2_pad.shape[0]
    r = FILT_ORDER
    c2 = 2 * D_HYENA
    steps = n // tt
    zero = jnp.zeros((r, r), F32)
    blockdiag = lambda w: jnp.concatenate([jnp.concatenate([w[0], zero], axis=1),
                                           jnp.concatenate([zero, w[1]], axis=1)], axis=0)
    cat = lambda v: jnp.concatenate([v[0], v[1]], axis=-1).reshape(1, 2 * r)
    w1c = jnp.pad(jnp.concatenate([w1[0], w1[1]], axis=1), ((0, LANES - EMB_DIM), (0, 0)))
    zc = jnp.zeros((r, c2), F32)
    w4c = jnp.stack([jnp.concatenate([jnp.concatenate([w4[0][:, hf * c2:(hf + 1) * c2], zc], axis=1),
                                      jnp.concatenate([zc, w4[1][:, hf * c2:(hf + 1) * c2]], axis=1)], axis=0)
                     for hf in range(2)])
    full = lambda shp: pl.BlockSpec(shp, lambda i: (0, 0))
    return pl.pallas_call(
        functools.partial(_filter_kernel, half_steps=steps // 2),
        grid=(steps,),
        in_specs=[pl.BlockSpec((tt, LANES), lambda i: (i, 0)),
                  full((LANES, 2 * r)), full((1, 2 * r)), full((2 * r, 2 * r)), full((1, 2 * r)),
                  full((2 * r, 2 * r)), full((1, 2 * r)), full((1, 2 * r)),
                  pl.BlockSpec((None, 2 * r, 2 * c2), lambda i: (i // (steps // 2), 0, 0)),
                  full((1, D_HYENA))],
        out_specs=[pl.BlockSpec((2, tt, c2), lambda i: (0, i, 0)),
                   pl.BlockSpec((2, 1, c2), lambda i: (0, 0, 0))],
        out_shape=[jax.ShapeDtypeStruct((2, n, c2), F32),
                   jax.ShapeDtypeStruct((2, 1, c2), F32)],
        compiler_params=_params("arbitrary"),
        name="hyena_filter",
    )(z2_pad, w1c, cat(b1), blockdiag(w2), cat(b2), blockdiag(w3), cat(b3), cat(fr), w4c, neg_abs_delta)


def _stack_c(m):
    return np.block([[m.real, -m.imag], [m.imag, m.real]])


def _interleave(a, axis):
    h = a.shape[axis] // 2
    idx = np.arange(2 * h).reshape(2, h).T.reshape(-1)
    return np.take(a, idx, axis=axis)


@functools.lru_cache(maxsize=None)
def _dft_tables(seq):
    n = 2 * seq
    r1 = LANES
    r2h = seq // r1
    n1 = np.arange(r1)
    k2 = np.arange(r1)
    n2 = np.arange(r2h)
    e = (k2[None, :, None] * (n1[:, None, None] + r1 * n2[None, None, :])) % n
    t = np.exp(-2j * np.pi * e / n)
    g_fwd = np.concatenate([np.concatenate([t.real, -t.imag], axis=2),
                            np.concatenate([t.imag, t.real], axis=2)], axis=1)
    g_fwd = _interleave(g_fwd, 1)
    n2f = np.arange(2 * r2h)
    e = (k2[None, :, None] * (n1[:, None, None] + r1 * n2f[None, None, :])) % n
    tf = np.exp(-2j * np.pi * e / n)
    g_real = _interleave(np.concatenate([tf.real, tf.imag], axis=1), 1)
    f = np.exp(-2j * np.pi * ((n1[:, None] * n1[None, :]) % r1) / r1)
    m3 = _interleave(_stack_c(f), 1)
    e = (n1[None, :, None] * (r1 * n1[None, None, :] + k2[:, None, None])) % n
    einv = np.exp(2j * np.pi * e / n)
    g_inv = np.concatenate([np.concatenate([einv.real, -einv.imag], axis=2),
                            np.concatenate([einv.imag, einv.real], axis=2)], axis=1)
    g_inv = _interleave(g_inv, 1)
    d = np.exp(2j * np.pi * ((n2[:, None] * k2[None, :]) % r1) / r1)
    m5 = _interleave(_stack_c(d), 1)
    tabs = dict(g_fwd=g_fwd, g_real=g_real, m3=m3, g_inv=g_inv, m5=m5)
    return {k: v.astype(np.float32) for k, v in tabs.items()}


def _pack(y):
    return pltpu.bitcast(y.astype(BF16), jnp.uint32)


def _unpack(w):
    return pltpu.bitcast(w, BF16)


GRP = 8


def _filt_stage1_kernel(x_ref, s_ref, g_ref, o_ref):
    scale = 1.0 / s_ref[...]
    half = x_ref.shape[0] // 2
    for j in range(GRP):
        x = jnp.concatenate([x_ref[:half, j, :], x_ref[half:, j, :]], axis=0)
        y = jnp.dot(g_ref[j], (x * scale).astype(BF16), preferred_element_type=F32)
        o_ref[j] = _pack(y)


def _filt_stage2_kernel(b_ref, m_ref, o_ref, *, inv_n):
    r1 = o_ref.shape[2]
    for j in range(GRP):
        x = jnp.dot(m_ref[...], _unpack(b_ref[:, j, :]), preferred_element_type=F32) * inv_n
        o_ref[0, j] = x[:r1].astype(o_ref.dtype)
        o_ref[1, j] = x[r1:].astype(o_ref.dtype)


def filter_spectrum(kern, asum, tabs):
    c = D_HYENA
    r1 = LANES
    nl = kern.shape[0]
    r2 = kern.shape[1] // r1
    x4 = kern.reshape(nl, r2, r1, 2 * c)
    s1 = pl.pallas_call(
        _filt_stage1_kernel,
        grid=(nl, 2, r1 // GRP),
        in_specs=[pl.BlockSpec((None, r2, GRP, c), lambda l, o, i: (l, 0, i, o)),
                  pl.BlockSpec((None, 1, c), lambda l, o, i: (l, 0, o)),
                  pl.BlockSpec((GRP, 2 * r1, r2), lambda l, o, i: (i, 0, 0))],
        out_specs=pl.BlockSpec((None, None, GRP, r1, c), lambda l, o, i: (l, o, i, 0, 0)),
        out_shape=jax.ShapeDtypeStruct((nl, 2, r1, r1, c), jnp.uint32),
        compiler_params=_params("parallel", "parallel", "parallel"),
        name="filt_dft1",
    )(x4, asum, tabs["g_real"])
    return pl.pallas_call(
        functools.partial(_filt_stage2_kernel, inv_n=1.0 / kern.shape[1]),
        grid=(nl, 2, r1 // GRP),
        in_specs=[pl.BlockSpec((None, None, r1, GRP, c), lambda l, o, i: (l, o, 0, i, 0)),
                  pl.BlockSpec((2 * r1, 2 * r1), lambda l, o, i: (0, 0))],
        out_specs=pl.BlockSpec((None, None, 2, GRP, r1, c), lambda l, o, i: (l, o, 0, i, 0, 0)),
        out_shape=jax.ShapeDtypeStruct((nl, 2, 2, r1, r1, c), BF16),
        compiler_params=_params("parallel", "parallel", "parallel"),
        name="filt_dft2",
    )(s1, tabs["m3"])


def _conv_stage1_kernel(x_ref, g_ref, o_ref):
    for j in range(GRP):
        zin = jnp.concatenate([x_ref[0, :, j, :], x_ref[1, :, j, :]], axis=0).astype(BF16)
        o_ref[j] = _pack(jnp.dot(g_ref[j], zin, preferred_element_type=F32))


def _conv_stage2_kernel(b_ref, kf_ref, m3_ref, ginv_ref, o_ref):
    r1 = o_ref.shape[1]
    for j in range(GRP):
        x = jnp.dot(m3_ref[...], _unpack(b_ref[:, j, :]), preferred_element_type=F32)
        xr, xi = x[:r1], x[r1:]
        kr, ki = kf_ref[0, j].astype(F32), kf_ref[1, j].astype(F32)
        y = jnp.concatenate([xr * kr - xi * ki, xr * ki + xi * kr], axis=0).astype(BF16)
        o_ref[j] = _pack(jnp.dot(ginv_ref[j], y, preferred_element_type=F32))


def _conv_stage3_mid_kernel(q_ref, m5_ref, u_ref, x1_ref, d_ref, g_ref, o_ref, s1_ref):
    half = u_ref.shape[1]
    d = d_ref[...]
    for j in range(GRP):
        y = jnp.dot(m5_ref[...], _unpack(q_ref[:, j, :]), preferred_element_type=F32)
        zs = []
        for bi in range(2):
            conv = y[bi * half:(bi + 1) * half] + u_ref[bi, :, j, :] * d
            z = x1_ref[bi, :, j, :] * conv
            o_ref[bi, :, j, :] = z
            zs.append(z)
        zin = jnp.concatenate(zs, axis=0).astype(BF16)
        s1_ref[j] = _pack(jnp.dot(g_ref[j], zin, preferred_element_type=F32))


def _conv_stage3_last_kernel(q_ref, m5_ref, u_ref, x2_ref, gate_ref, d_ref, o_ref):
    half = u_ref.shape[1]
    d = d_ref[...]
    for j in range(GRP):
        y = jnp.dot(m5_ref[...], _unpack(q_ref[:, j, :]), preferred_element_type=F32)
        for bi in range(2):
            g = gate_ref[bi, :, j, :]
            conv = y[bi * half:(bi + 1) * half] + u_ref[bi, :, j, :] * d
            o_ref[bi, :, j, :] = x2_ref[bi, :, j, :] * conv * (g / (1.0 + jnp.exp(-g)))


def _conv_dft1(src4, blk, tabs):
    c = D_HYENA
    r1 = LANES
    b, r2h = src4.shape[0], src4.shape[1]
    return pl.pallas_call(
        _conv_stage1_kernel,
        grid=(r1 // GRP,),
        in_specs=[pl.BlockSpec((b, r2h, GRP, c), lambda i: (0, 0, i, blk)),
                  pl.BlockSpec((GRP, 2 * r1, 2 * r2h), lambda i: (i, 0, 0))],
        out_specs=pl.BlockSpec((GRP, r1, c), lambda i: (i, 0, 0)),
        out_shape=jax.ShapeDtypeStruct((r1, r1, c), jnp.uint32),
        compiler_params=_params("parallel"),
        name="conv_dft1",
    )(src4, tabs["g_fwd"])


def _conv_dft2(s1, spec, layer, order, tabs):
    c = D_HYENA
    r1 = LANES
    return pl.pallas_call(
        _conv_stage2_kernel,
        grid=(r1 // GRP,),
        in_specs=[pl.BlockSpec((r1, GRP, c), lambda i: (0, i, 0)),
                  pl.BlockSpec((None, None, 2, GRP, r1, c), lambda i: (layer, order, 0, i, 0, 0)),
                  pl.BlockSpec((2 * r1, 2 * r1), lambda i: (0, 0)),
                  pl.BlockSpec((GRP, 2 * r1, 2 * r1), lambda i: (i, 0, 0))],
        out_specs=pl.BlockSpec((GRP, r1, c), lambda i: (i, 0, 0)),
        out_shape=jax.ShapeDtypeStruct((r1, r1, c), jnp.uint32),
        compiler_params=_params("parallel"),
        name="conv_dft2",
    )(s1, spec, tabs["m3"], tabs["g_inv"])


def hyena_branch(csc, hy3, spec, layer, filt_bias, tabs, seq):
    c = D_HYENA
    r1 = LANES
    r2h = seq // r1
    b = csc.shape[0]
    steps = r1 // GRP
    csc4 = csc.reshape(b, r2h, r1, 3 * c)
    hy4 = hy3.reshape(b, r2h, r1, HY_COLS)
    qspec = pl.BlockSpec((r1, GRP, c), lambda i: (0, i, 0))
    m5spec = pl.BlockSpec((2 * r2h, 2 * r1), lambda i: (0, 0))
    dspec = pl.BlockSpec((1, c), lambda i: (0, 0))
    tblk = lambda blk: pl.BlockSpec((b, r2h, GRP, c), lambda i: (0, 0, i, blk))

    q1 = _conv_dft2(_conv_dft1(csc4, 0, tabs), spec, layer, 0, tabs)
    z, s1 = pl.pallas_call(
        _conv_stage3_mid_kernel,
        grid=(steps,),
        in_specs=[qspec, m5spec, tblk(0), tblk(1), dspec,
                  pl.BlockSpec((GRP, 2 * r1, 2 * r2h), lambda i: (i, 0, 0))],
        out_specs=[tblk(0), pl.BlockSpec((GRP, r1, c), lambda i: (i, 0, 0))],
        out_shape=[jax.ShapeDtypeStruct((b, r2h, r1, c), F32),
                   jax.ShapeDtypeStruct((r1, r1, c), jnp.uint32)],
        compiler_params=_params("parallel"),
        name="conv_idft_mid",
    )(q1, tabs["m5"], csc4, csc4, filt_bias[0:1], tabs["g_fwd"])

    q2 = _conv_dft2(s1, spec, layer, 1, tabs)
    y = pl.pallas_call(
        _conv_stage3_last_kernel,
        grid=(steps,),
        in_specs=[qspec, m5spec, tblk(0), tblk(2), tblk(HY_PROJ // c), dspec],
        out_specs=tblk(0),
        out_shape=jax.ShapeDtypeStruct((b, r2h, r1, c), F32),
        compiler_params=_params("parallel"),
        name="conv_idft_last",
    )(q2, tabs["m5"], z, csc4, hy4, filt_bias[1:2])
    return y.reshape(b, seq, c)


def _out_proj_kernel(yh_ref, ya_ref, w_ref, h_ref, g_ref, o_ref, *, final):
    k = yh_ref.shape[1]
    acc = jnp.dot(yh_ref[...].astype(BF16), w_ref[:k, :], preferred_element_type=F32)
    acc = acc + jnp.dot(ya_ref[...].astype(BF16), w_ref[k:, :], preferred_element_type=F32)
    hn = h_ref[...] + acc
    if final:
        ms = jnp.mean(hn * hn, axis=-1, keepdims=True)
        hn = hn * lax.rsqrt(ms + NORM_EPS) * g_ref[...]
    o_ref[...] = hn


def out_proj(yh, ya, w_bf16, h2, g, final, tm=512):
    m, d = h2.shape
    k = yh.shape[1]
    return pl.pallas_call(
        functools.partial(_out_proj_kernel, final=final),
        grid=(m // tm,),
        in_specs=[pl.BlockSpec((tm, k), lambda i: (i, 0)),
                  pl.BlockSpec((tm, k), lambda i: (i, 0)),
                  pl.BlockSpec((2 * k, d), lambda i: (0, 0)),
                  pl.BlockSpec((tm, d), lambda i: (i, 0)),
                  pl.BlockSpec((1, d), lambda i: (0, 0))],
        out_specs=pl.BlockSpec((tm, d), lambda i: (i, 0)),
        out_shape=jax.ShapeDtypeStruct((m, d), F32),
        compiler_params=_params("parallel"),
        name="out_proj",
    )(yh, ya, w_bf16, h2, g.reshape(1, d))


def _rope_tables(seq):
    inv_freq = 1.0 / (ROPE_THETA ** (jnp.arange(0, QK_DIM, 2, dtype=F32) / QK_DIM))
    ang = jnp.arange(seq, dtype=F32)[:, None] * inv_freq[None, :]
    ang = jnp.concatenate([ang, ang, ang, ang], axis=-1)
    sign = jnp.where((jnp.arange(LANES) % QK_DIM) < QK_DIM // 2, -1.0, 1.0).astype(F32)
    return jnp.cos(ang), jnp.sin(ang) * sign[None, :]


def _filter_features(seq):
    t = jnp.linspace(0.0, 1.0, seq, dtype=F32)[:, None]
    bands = (EMB_DIM - 1) // 2
    wpos = 2.0 * math.pi * jnp.arange(seq, dtype=F32) / seq
    fb = jnp.linspace(1e-4, bands - 1, bands, dtype=F32)
    ph = wpos[:, None] * fb[None, :]
    z = jnp.concatenate([t, jnp.cos(ph), -jnp.sin(ph)], axis=-1)
    z2 = jnp.concatenate([z, jnp.roll(z[::-1], 1, axis=0)], axis=0)
    z_pad = jnp.pad(z2, ((0, 0), (0, LANES - EMB_DIM)))
    max_decay = math.log(DECAY_TARGET) / FAST_DECAY_PCT
    min_decay = math.log(DECAY_TARGET) / SLOW_DECAY_PCT
    deltas = jnp.linspace(min_decay, max_decay, D_HYENA, dtype=F32)
    return z_pad, (-jnp.abs(deltas))[None, :]


def kernel(x, norm_g, w_in, conv_w, conv_b, filt_w1, filt_b1, filt_w2, filt_b2, filt_w3, filt_b3,
           filt_freq, filt_w4, filt_bias, lam_q1, lam_k1, lam_q2, lam_k2, subln_g, w_out, final_g):
    b, seq, d = x.shape
    depth = w_in.shape[0]
    m = b * seq
    tabs = {k: jnp.asarray(v).astype(BF16) for k, v in _dft_tables(seq).items()}
    cos_t, sin_t = _rope_tables(seq)
    z_pad, neg_abs_delta = _filter_features(seq)

    kern, asum = filter_taps(z_pad, filt_w1, filt_b1, filt_w2, filt_b2, filt_w3, filt_b3, filt_freq, filt_w4,
                             neg_abs_delta)
    spec = filter_spectrum(kern, asum, tabs)

    h = x.reshape(m, d)
    for layer in range(depth):
        lam_init = 0.8 - 0.6 * math.exp(-0.3 * layer)
        hy, q0, q1, kr, vt, ag = in_proj(h, norm_g[layer], w_in[layer].astype(BF16), cos_t, sin_t, b, seq)
        hy3 = hy.reshape(b, seq, HY_COLS)
        sh = (b, seq, D_ATTN)
        y_a = attention(q0.reshape(sh), q1.reshape(sh), kr.reshape(sh), vt, ag.reshape(sh),
                        lam_q1[layer], lam_k1[layer], lam_q2[layer], lam_k2[layer], subln_g[layer], lam_init)

        csc = short_conv(hy3, conv_w[layer], conv_b[layer])
        y_h = hyena_branch(csc, hy3, spec, layer, filt_bias[layer], tabs, seq)

        h = out_proj(y_h.reshape(m, D_HYENA), y_a.reshape(m, D_ATTN), w_out[layer].astype(BF16), h,
                     final_g, final=(layer == depth - 1))
    return h.reshape(b, seq, d)
```

```python
import functools
import math

import numpy as np
import jax
import jax.numpy as jnp
from jax import lax
from jax.experimental import pallas as pl
from jax.experimental.pallas import tpu as pltpu

F32 = jnp.float32
BF16 = jnp.bfloat16

D_MODEL = 1024
D_HYENA = 512
D_ATTN = 512
N_HEADS = 4
V_DIM = 128
QK_DIM = 64
HY_PROJ = 3 * D_HYENA
D_IN = 4096
EMB_DIM = 33
FILT_ORDER = 64
DECAY_TARGET = 1e-2
FAST_DECAY_PCT = 0.3
SLOW_DECAY_PCT = 1.5
ROPE_THETA = 10000.0
NORM_EPS = 1e-6
SUBLN_EPS = 1e-5

LANES = 128
VMEM_LIMIT = 48 * 1024 * 1024


def _params(*sem):
    return pltpu.CompilerParams(dimension_semantics=sem, vmem_limit_bytes=VMEM_LIMIT)


VT_ROWS = V_DIM + 16
LOG2E = 1.4426950408889634
HY_COLS = HY_PROJ + D_HYENA


def _rope(t, cos, sin, first):
    rot = jnp.where(first, pltpu.roll(t, LANES - QK_DIM // 2, axis=1), pltpu.roll(t, QK_DIM // 2, axis=1))
    return t * cos + rot * sin


def _in_proj_kernel(x_ref, g_ref, w_ref, cos_ref, sin_ref, hy_ref, q0_ref, q1_ref, ko_ref, vt_ref, ag_ref,
                    u_ref, qk_ref):
    j = pl.program_id(1)
    w = N_HEADS * LANES

    def project():
        return jnp.dot(u_ref[...], w_ref[...], preferred_element_type=F32)

    @pl.when(j == 0)
    def _():
        x = x_ref[...]
        ms = jnp.mean(x * x, axis=-1, keepdims=True)
        u_ref[...] = (x * lax.rsqrt(ms + NORM_EPS) * g_ref[...]).astype(BF16)
        hy_ref[...] = project()

    @pl.when(j == 1)
    def _():
        hy_ref[...] = project()

    @pl.when(j == 2)
    def _():
        qk_ref[...] = project()

    @pl.when(j == 3)
    def _():
        acc = project()
        cos = cos_ref[...]
        sin = sin_ref[...]
        lane = lax.broadcasted_iota(jnp.int32, cos.shape, 1)
        first = (lane & (QK_DIM - 1)) < QK_DIM // 2
        map0 = lane < QK_DIM
        for h in range(N_HEADS):
            sl = slice(h * LANES, (h + 1) * LANES)
            r = _rope(qk_ref[:, sl], cos, sin, first) * (QK_DIM ** -0.5 * LOG2E)
            q0_ref[:, sl] = jnp.where(map0, r, 0.0).astype(BF16)
            q1_ref[:, sl] = jnp.where(map0, 0.0, r).astype(BF16)
            ko_ref[:, sl] = _rope(qk_ref[:, w + h * LANES:w + (h + 1) * LANES], cos, sin, first).astype(BF16)
        tk = vt_ref.shape[3]
        for h in range(N_HEADS):
            for c in range(vt_ref.shape[1]):
                vt_ref[h, c, :V_DIM, :] = acc[c * tk:(c + 1) * tk, h * LANES:(h + 1) * LANES].T.astype(BF16)
                vt_ref[h, c, V_DIM:, :] = jnp.ones((VT_ROWS - V_DIM, tk), BF16)
        ag_ref[...] = acc[:, w:]


def in_proj(x2, g, w_bf16, cos_t, sin_t, batch, seq, tm=1024, tk=512):
    m, d = x2.shape
    tn = 2 * N_HEADS * LANES
    assert w_bf16.shape[1] == 4 * tn and HY_COLS == 2 * tn
    w = N_HEADS * LANES
    spb = seq // tm
    half = jax.ShapeDtypeStruct((m, w), BF16)
    hspec = pl.BlockSpec((tm, w), lambda i, j: (i, 0))
    return pl.pallas_call(
        _in_proj_kernel,
        grid=(m // tm, 4),
        in_specs=[pl.BlockSpec((tm, d), lambda i, j: (i, 0)),
                  pl.BlockSpec((1, d), lambda i, j: (0, 0)),
                  pl.BlockSpec((d, tn), lambda i, j: (0, j)),
                  pl.BlockSpec((tm, LANES), lambda i, j: (i % spb, 0)),
                  pl.BlockSpec((tm, LANES), lambda i, j: (i % spb, 0))],
        out_specs=[pl.BlockSpec((tm, tn), lambda i, j: (i, jnp.minimum(j, 1))),
                   hspec, hspec, hspec,
                   pl.BlockSpec((None, N_HEADS, tm // tk, VT_ROWS, tk), lambda i, j: (i // spb, 0, i % spb, 0, 0)),
                   hspec],
        out_shape=[jax.ShapeDtypeStruct((m, HY_COLS), F32), half, half, half,
                   jax.ShapeDtypeStruct((batch, N_HEADS, seq // tk, VT_ROWS, tk), BF16),
                   jax.ShapeDtypeStruct((m, w), F32)],
        scratch_shapes=[pltpu.VMEM((tm, d), BF16), pltpu.VMEM((tm, tn), F32)],
        compiler_params=_params("parallel", "arbitrary"),
        name="in_proj",
    )(x2, g.reshape(1, d), w_bf16, cos_t, sin_t)


def _attn_kernel(q0_ref, q1_ref, k_ref, vt_ref, gate_ref, lq1_ref, lk1_ref, lq2_ref, lk2_ref,
                 subg_ref, o_ref, *, lam_init):
    tq = q0_ref.shape[0]
    nk, _, tk = vt_ref.shape
    q0 = q0_ref[...]
    q1 = q1_ref[...]

    def scores(q, kj):
        return lax.dot_general(kj, q, (((1,), (1,)), ((), ())), preferred_element_type=F32)

    def local(st):
        mx = jnp.max(st, axis=0, keepdims=True)
        return mx, jnp.exp2((st - mx).astype(BF16))

    def merge(m, acc, mx, pv):
        m_new = jnp.maximum(m, mx)
        return m_new, jnp.exp2(m - m_new) * acc + jnp.exp2(mx - m_new) * pv

    neg = jnp.full((1, tq), -1e30, F32)
    zacc = jnp.zeros((vt_ref.shape[1], tq), F32)
    m0, a0, m1, a1 = neg, zacc, neg, zacc
    k0 = k_ref[0:tk, :]
    nxt = (scores(q0, k0), scores(q1, k0))
    for j in range(nk):
        st0, st1 = nxt
        if j + 1 < nk:
            kn = k_ref[(j + 1) * tk:(j + 2) * tk, :]
            nxt = (scores(q0, kn), scores(q1, kn))
        vtj = vt_ref[j]
        mx0, p0 = local(st0)
        mx1, p1 = local(st1)
        m0, a0 = merge(m0, a0, mx0, jnp.dot(vtj, p0, preferred_element_type=F32))
        m1, a1 = merge(m1, a1, mx1, jnp.dot(vtj, p1, preferred_element_type=F32))
    l0, l1 = a0[V_DIM:V_DIM + 1], a1[V_DIM:V_DIM + 1]
    a0, a1 = a0[:V_DIM], a1[:V_DIM]

    lam = (jnp.exp(jnp.sum(lq1_ref[...] * lk1_ref[...], axis=-1, keepdims=True))
           - jnp.exp(jnp.sum(lq2_ref[...] * lk2_ref[...], axis=-1, keepdims=True)) + lam_init)
    ot = a0 / l0 - lam * (a1 / l1)
    ot = ot * lax.rsqrt(jnp.mean(ot * ot, axis=0, keepdims=True) + SUBLN_EPS)
    o = ot.T * subg_ref[...] * (1.0 - lam_init)
    g = gate_ref[...]
    o_ref[...] = (o * (g / (1.0 + jnp.exp(-g)))).astype(o_ref.dtype)


def attention(q0, q1, k, vt, gate, lq1, lk1, lq2, lk2, subg, lam_init, tq=512):
    b, s, _ = q0.shape
    nk, tk = vt.shape[2], vt.shape[4]
    qspec = pl.BlockSpec((None, tq, LANES), lambda bi, h, qi: (bi, qi, h))
    small = pl.BlockSpec((1, QK_DIM), lambda bi, h, qi: (0, 0))
    return pl.pallas_call(
        functools.partial(_attn_kernel, lam_init=lam_init),
        grid=(b, N_HEADS, s // tq),
        in_specs=[qspec, qspec,
                  pl.BlockSpec((None, s, LANES), lambda bi, h, qi: (bi, 0, h)),
                  pl.BlockSpec((None, None, nk, VT_ROWS, tk), lambda bi, h, qi: (bi, h, 0, 0, 0)),
                  qspec,
                  small, small, small, small,
                  pl.BlockSpec((1, V_DIM), lambda bi, h, qi: (0, 0))],
        out_specs=pl.BlockSpec((None, tq, LANES), lambda bi, h, qi: (bi, qi, h)),
        out_shape=jax.ShapeDtypeStruct((b, s, D_ATTN), BF16),
        compiler_params=_params("parallel", "parallel", "arbitrary"),
        name="diff_attn",
    )(q0, q1, k, vt, gate, lq1.reshape(1, -1), lk1.reshape(1, -1), lq2.reshape(1, -1),
      lk2.reshape(1, -1), subg.reshape(1, -1))


def _short_conv_kernel(u_ref, w_ref, b_ref, o_ref):
    u = u_ref[...]
    n = u.shape[0]
    row = lax.broadcasted_iota(jnp.int32, u.shape, 0)
    prev = jnp.where(row == 0, 0.0, pltpu.roll(u, 1, axis=0))
    nxt = jnp.where(row == n - 1, 0.0, pltpu.roll(u, n - 1, axis=0))
    o_ref[...] = prev * w_ref[0:1, :] + u * w_ref[1:2, :] + nxt * w_ref[2:3, :] + b_ref[...]


def short_conv(hy3, conv_w, conv_b):
    b, s, _ = hy3.shape
    nblk = HY_PROJ // LANES
    return pl.pallas_call(
        _short_conv_kernel,
        grid=(b, nblk),
        in_specs=[pl.BlockSpec((None, s, LANES), lambda bi, j: (bi, 0, j)),
                  pl.BlockSpec((3, LANES), lambda bi, j: (0, j)),
                  pl.BlockSpec((1, LANES), lambda bi, j: (0, j))],
        out_specs=pl.BlockSpec((None, s, LANES), lambda bi, j: (bi, 0, j)),
        out_shape=jax.ShapeDtypeStruct((b, s, HY_PROJ), F32),
        compiler_params=_params("parallel", "parallel"),
        name="short_conv",
    )(hy3, conv_w, conv_b.reshape(1, -1))


def _filter_kernel(z_ref, w1_ref, b1_ref, w2_ref, b2_ref, w3_ref, b3_ref, fr_ref, w4_ref, nd_ref,
                   k_ref, asum_ref, *, half_steps):
    i = pl.program_id(0)
    hi = lax.Precision.HIGHEST
    z = z_ref[...]
    fr = fr_ref[...]
    h = jnp.sin(fr * (jnp.dot(z, w1_ref[...], precision=hi, preferred_element_type=F32) + b1_ref[...]))
    h = jnp.sin(fr * (jnp.dot(h, w2_ref[...], precision=hi, preferred_element_type=F32) + b2_ref[...]))
    h = jnp.sin(fr * (jnp.dot(h, w3_ref[...], precision=hi, preferred_element_type=F32) + b3_ref[...]))
    hf = jnp.dot(h.astype(BF16), w4_ref[...].astype(BF16), preferred_element_type=F32)
    decay = jnp.exp(z[:, 0:1] * nd_ref[...])
    tt = z.shape[0]
    row = lax.broadcasted_iota(jnp.int32, (tt, D_HYENA), 0)
    centre = jnp.logical_and(row == 0, i == half_steps)

    @pl.when(i == 0)
    def _():
        asum_ref[...] = jnp.zeros_like(asum_ref)

    for layer in range(2):
        for o in range(2):
            sl = slice(o * D_HYENA, (o + 1) * D_HYENA)
            src = slice((2 * layer + o) * D_HYENA, (2 * layer + o + 1) * D_HYENA)
            kg = jnp.where(centre, 0.0, hf[:, src] * decay)
            k_ref[layer, :, sl] = kg
            asum_ref[layer, :, sl] += jnp.sum(jnp.abs(kg), axis=0, keepdims=True)


def filter_taps(z2_pad, w1, b1, w2, b2, w3, b3, fr, w4, neg_abs_delta, tt=512):
    assert w1.shape[0] == 2, "the filter MLP packs exactly two layers into the 128 lanes"
    n = z2_pad.shape[0]
    r = FILT_ORDER
    c2 = 2 * D_HYENA
    steps = n // tt
    zero = jnp.zeros((r, r), F32)
    blockdiag = lambda w: jnp.concatenate([jnp.concatenate([w[0], zero], axis=1),
                                           jnp.concatenate([zero, w[1]], axis=1)], axis=0)
    cat = lambda v: jnp.concatenate([v[0], v[1]], axis=-1).reshape(1, 2 * r)
    w1c = jnp.pad(jnp.concatenate([w1[0], w1[1]], axis=1), ((0, LANES - EMB_DIM), (0, 0)))
    zc = jnp.zeros((r, c2), F32)
    w4c = jnp.stack([jnp.concatenate([jnp.concatenate([w4[0][:, hf * c2:(hf + 1) * c2], zc], axis=1),
                                      jnp.concatenate([zc, w4[1][:, hf * c2:(hf + 1) * c2]], axis=1)], axis=0)
                     for hf in range(2)])
    full = lambda shp: pl.BlockSpec(shp, lambda i: (0, 0))
    return pl.pallas_call(
        functools.partial(_filter_kernel, half_steps=steps // 2),
        grid=(steps,),
        in_specs=[pl.BlockSpec((tt, LANES), lambda i: (i, 0)),
                  full((LANES, 2 * r)), full((1, 2 * r)), full((2 * r, 2 * r)), full((1, 2 * r)),
                  full((2 * r, 2 * r)), full((1, 2 * r)), full((1, 2 * r)),
                  pl.BlockSpec((None, 2 * r, 2 * c2), lambda i: (i // (steps // 2), 0, 0)),
                  full((1, D_HYENA))],
        out_specs=[pl.BlockSpec((2, tt, c2), lambda i: (0, i, 0)),
                   pl.BlockSpec((2, 1, c2), lambda i: (0, 0, 0))],
        out_shape=[jax.ShapeDtypeStruct((2, n, c2), F32),
                   jax.ShapeDtypeStruct((2, 1, c2), F32)],
        compiler_params=_params("arbitrary"),
        name="hyena_filter",
    )(z2_pad, w1c, cat(b1), blockdiag(w2), cat(b2), blockdiag(w3), cat(b3), cat(fr), w4c, neg_abs_delta)


def _stack_c(m):
    return np.block([[m.real, -m.imag], [m.imag, m.real]])


def _interleave(a, axis):
    h = a.shape[axis] // 2
    idx = np.arange(2 * h).reshape(2, h).T.reshape(-1)
    return np.take(a, idx, axis=axis)


@functools.lru_cache(maxsize=None)
def _dft_tables(seq):
    n = 2 * seq
    r1 = LANES
    r2h = seq // r1
    n1 = np.arange(r1)
    k2 = np.arange(r1)
    n2 = np.arange(r2h)
    e = (k2[None, :, None] * (n1[:, None, None] + r1 * n2[None, None, :])) % n
    t = np.exp(-2j * np.pi * e / n)
    g_fwd = np.concatenate([np.concatenate([t.real, -t.imag], axis=2),
                            np.concatenate([t.imag, t.real], axis=2)], axis=1)
    g_fwd = _interleave(g_fwd, 1)
    n2f = np.arange(2 * r2h)
    e = (k2[None, :, None] * (n1[:, None, None] + r1 * n2f[None, None, :])) % n
    tf = np.exp(-2j * np.pi * e / n)
    g_real = _interleave(np.concatenate([tf.real, tf.imag], axis=1), 1)
    f = np.exp(-2j * np.pi * ((n1[:, None] * n1[None, :]) % r1) / r1)
    m3 = _interleave(_stack_c(f), 1)
    e = (n1[None, :, None] * (r1 * n1[None, None, :] + k2[:, None, None])) % n
    einv = np.exp(2j * np.pi * e / n)
    g_inv = np.concatenate([np.concatenate([einv.real, -einv.imag], axis=2),
                            np.concatenate([einv.imag, einv.real], axis=2)], axis=1)
    g_inv = _interleave(g_inv, 1)
    d = np.exp(2j * np.pi * ((n2[:, None] * k2[None, :]) % r1) / r1)
    m5 = _interleave(_stack_c(d), 1)
    tabs = dict(g_fwd=g_fwd, g_real=g_real, m3=m3, g_inv=g_inv, m5=m5)
    return {k: v.astype(np.float32) for k, v in tabs.items()}


def _pack(y):
    return pltpu.bitcast(y.astype(BF16), jnp.uint32)


def _unpack(w):
    return pltpu.bitcast(w, BF16)


GRP = 8


def _filt_stage1_kernel(x_ref, s_ref, g_ref, o_ref):
    scale = 1.0 / s_ref[...]
    half = x_ref.shape[0] // 2
    for j in range(GRP):
        x = jnp.concatenate([x_ref[:half, j, :], x_ref[half:, j, :]], axis=0)
        y = jnp.dot(g_ref[j], (x * scale).astype(BF16), preferred_element_type=F32)
        o_ref[j] = _pack(y)


def _filt_stage2_kernel(b_ref, m_ref, o_ref, *, inv_n):
    r1 = o_ref.shape[2]
    for j in range(GRP):
        x = jnp.dot(m_ref[...], _unpack(b_ref[:, j, :]), preferred_element_type=F32) * inv_n
        o_ref[0, j] = x[:r1].astype(o_ref.dtype)
        o_ref[1, j] = x[r1:].astype(o_ref.dtype)


def filter_spectrum(kern, asum, tabs):
    c = D_HYENA
    r1 = LANES
    nl = kern.shape[0]
    r2 = kern.shape[1] // r1
    x4 = kern.reshape(nl, r2, r1, 2 * c)
    s1 = pl.pallas_call(
        _filt_stage1_kernel,
        grid=(nl, 2, r1 // GRP),
        in_specs=[pl.BlockSpec((None, r2, GRP, c), lambda l, o, i: (l, 0, i, o)),
                  pl.BlockSpec((None, 1, c), lambda l, o, i: (l, 0, o)),
                  pl.BlockSpec((GRP, 2 * r1, r2), lambda l, o, i: (i, 0, 0))],
        out_specs=pl.BlockSpec((None, None, GRP, r1, c), lambda l, o, i: (l, o, i, 0, 0)),
        out_shape=jax.ShapeDtypeStruct((nl, 2, r1, r1, c), jnp.uint32),
        compiler_params=_params("parallel", "parallel", "parallel"),
        name="filt_dft1",
    )(x4, asum, tabs["g_real"])
    return pl.pallas_call(
        functools.partial(_filt_stage2_kernel, inv_n=1.0 / kern.shape[1]),
        grid=(nl, 2, r1 // GRP),
        in_specs=[pl.BlockSpec((None, None, r1, GRP, c), lambda l, o, i: (l, o, 0, i, 0)),
                  pl.BlockSpec((2 * r1, 2 * r1), lambda l, o, i: (0, 0))],
        out_specs=pl.BlockSpec((None, None, 2, GRP, r1, c), lambda l, o, i: (l, o, 0, i, 0, 0)),
        out_shape=jax.ShapeDtypeStruct((nl, 2, 2, r1, r1, c), BF16),
        compiler_params=_params("parallel", "parallel", "parallel"),
        name="filt_dft2",
    )(s1, tabs["m3"])


def _conv_stage1_kernel(x_ref, g_ref, o_ref):
    for j in range(GRP):
        zin = jnp.concatenate([x_ref[0, :, j, :], x_ref[1, :, j, :]], axis=0).astype(BF16)
        o_ref[j] = _pack(jnp.dot(g_ref[j], zin, preferred_element_type=F32))


def _conv_stage2_kernel(b_ref, kf_ref, m3_ref, ginv_ref, o_ref):
    r1 = o_ref.shape[1]
    for j in range(GRP):
        x = jnp.dot(m3_ref[...], _unpack(b_ref[:, j, :]), preferred_element_type=F32)
        xr, xi = x[:r1], x[r1:]
        kr, ki = kf_ref[0, j].astype(F32), kf_ref[1, j].astype(F32)
        y = jnp.concatenate([xr * kr - xi * ki, xr * ki + xi * kr], axis=0).astype(BF16)
        o_ref[j] = _pack(jnp.dot(ginv_ref[j], y, preferred_element_type=F32))


def _conv_stage3_mid_kernel(q_ref, m5_ref, u_ref, x1_ref, d_ref, g_ref, o_ref, s1_ref):
    half = u_ref.shape[1]
    d = d_ref[...]
    for j in range(GRP):
        y = jnp.dot(m5_ref[...], _unpack(q_ref[:, j, :]), preferred_element_type=F32)
        zs = []
        for bi in range(2):
            conv = y[bi * half:(bi + 1) * half] + u_ref[bi, :, j, :] * d
            z = x1_ref[bi, :, j, :] * conv
            o_ref[bi, :, j, :] = z
            zs.append(z)
        zin = jnp.concatenate(zs, axis=0).astype(BF16)
        s1_ref[j] = _pack(jnp.dot(g_ref[j], zin, preferred_element_type=F32))


def _conv_stage3_last_kernel(q_ref, m5_ref, u_ref, x2_ref, gate_ref, d_ref, o_ref):
    half = u_ref.shape[1]
    d = d_ref[...]
    for j in range(GRP):
        y = jnp.dot(m5_ref[...], _unpack(q_ref[:, j, :]), preferred_element_type=F32)
        for bi in range(2):
            g = gate_ref[bi, :, j, :]
            conv = y[bi * half:(bi + 1) * half] + u_ref[bi, :, j, :] * d
            o_ref[bi, :, j, :] = x2_ref[bi, :, j, :] * conv * (g / (1.0 + jnp.exp(-g)))


def _conv_dft1(src4, blk, tabs):
    c = D_HYENA
    r1 = LANES
    b, r2h = src4.shape[0], src4.shape[1]
    return pl.pallas_call(
        _conv_stage1_kernel,
        grid=(r1 // GRP,),
        in_specs=[pl.BlockSpec((b, r2h, GRP, c), lambda i: (0, 0, i, blk)),
                  pl.BlockSpec((GRP, 2 * r1, 2 * r2h), lambda i: (i, 0, 0))],
        out_specs=pl.BlockSpec((GRP, r1, c), lambda i: (i, 0, 0)),
        out_shape=jax.ShapeDtypeStruct((r1, r1, c), jnp.uint32),
        compiler_params=_params("parallel"),
        name="conv_dft1",
    )(src4, tabs["g_fwd"])


def _conv_dft2(s1, spec, layer, order, tabs):
    c = D_HYENA
    r1 = LANES
    return pl.pallas_call(
        _conv_stage2_kernel,
        grid=(r1 // GRP,),
        in_specs=[pl.BlockSpec((r1, GRP, c), lambda i: (0, i, 0)),
                  pl.BlockSpec((None, None, 2, GRP, r1, c), lambda i: (layer, order, 0, i, 0, 0)),
                  pl.BlockSpec((2 * r1, 2 * r1), lambda i: (0, 0)),
                  pl.BlockSpec((GRP, 2 * r1, 2 * r1), lambda i: (i, 0, 0))],
        out_specs=pl.BlockSpec((GRP, r1, c), lambda i: (i, 0, 0)),
        out_shape=jax.ShapeDtypeStruct((r1, r1, c), jnp.uint32),
        compiler_params=_params("parallel"),
        name="conv_dft2",
    )(s1, spec, tabs["m3"], tabs["g_inv"])


def hyena_branch(csc, hy3, spec, layer, filt_bias, tabs, seq):
    c = D_HYENA
    r1 = LANES
    r2h = seq // r1
    b = csc.shape[0]
    steps = r1 // GRP
    csc4 = csc.reshape(b, r2h, r1, 3 * c)
    hy4 = hy3.reshape(b, r2h, r1, HY_COLS)
    qspec = pl.BlockSpec((r1, GRP, c), lambda i: (0, i, 0))
    m5spec = pl.BlockSpec((2 * r2h, 2 * r1), lambda i: (0, 0))
    dspec = pl.BlockSpec((1, c), lambda i: (0, 0))
    tblk = lambda blk: pl.BlockSpec((b, r2h, GRP, c), lambda i: (0, 0, i, blk))

    q1 = _conv_dft2(_conv_dft1(csc4, 0, tabs), spec, layer, 0, tabs)
    z, s1 = pl.pallas_call(
        _conv_stage3_mid_kernel,
        grid=(steps,),
        in_specs=[qspec, m5spec, tblk(0), tblk(1), dspec,
                  pl.BlockSpec((GRP, 2 * r1, 2 * r2h), lambda i: (i, 0, 0))],
        out_specs=[tblk(0), pl.BlockSpec((GRP, r1, c), lambda i: (i, 0, 0))],
        out_shape=[jax.ShapeDtypeStruct((b, r2h, r1, c), F32),
                   jax.ShapeDtypeStruct((r1, r1, c), jnp.uint32)],
        compiler_params=_params("parallel"),
        name="conv_idft_mid",
    )(q1, tabs["m5"], csc4, csc4, filt_bias[0:1], tabs["g_fwd"])

    q2 = _conv_dft2(s1, spec, layer, 1, tabs)
    y = pl.pallas_call(
        _conv_stage3_last_kernel,
        grid=(steps,),
        in_specs=[qspec, m5spec, tblk(0), tblk(2), tblk(HY_PROJ // c), dspec],
        out_specs=tblk(0),
        out_shape=jax.ShapeDtypeStruct((b, r2h, r1, c), F32),
        compiler_params=_params("parallel"),
        name="conv_idft_last",
    )(q2, tabs["m5"], z, csc4, hy4, filt_bias[1:2])
    return y.reshape(b, seq, c)


def _out_proj_kernel(yh_ref, ya_ref, w_ref, h_ref, g_ref, o_ref, *, final):
    k = yh_ref.shape[1]
    acc = jnp.dot(yh_ref[...].astype(BF16), w_ref[:k, :], preferred_element_type=F32)
    acc = acc + jnp.dot(ya_ref[...].astype(BF16), w_ref[k:, :], preferred_element_type=F32)
    hn = h_ref[...] + acc
    if final:
        ms = jnp.mean(hn * hn, axis=-1, keepdims=True)
        hn = hn * lax.rsqrt(ms + NORM_EPS) * g_ref[...]
    o_ref[...] = hn


def out_proj(yh, ya, w_bf16, h2, g, final, tm=512):
    m, d = h2.shape
    k = yh.shape[1]
    return pl.pallas_call(
        functools.partial(_out_proj_kernel, final=final),
        grid=(m // tm,),
        in_specs=[pl.BlockSpec((tm, k), lambda i: (i, 0)),
                  pl.BlockSpec((tm, k), lambda i: (i, 0)),
                  pl.BlockSpec((2 * k, d), lambda i: (0, 0)),
                  pl.BlockSpec((tm, d), lambda i: (i, 0)),
                  pl.BlockSpec((1, d), lambda i: (0, 0))],
        out_specs=pl.BlockSpec((tm, d), lambda i: (i, 0)),
        out_shape=jax.ShapeDtypeStruct((m, d), F32),
        compiler_params=_params("parallel"),
        name="out_proj",
    )(yh, ya, w_bf16, h2, g.reshape(1, d))


def _rope_tables(seq):
    inv_freq = 1.0 / (ROPE_THETA ** (jnp.arange(0, QK_DIM, 2, dtype=F32) / QK_DIM))
    ang = jnp.arange(seq, dtype=F32)[:, None] * inv_freq[None, :]
    ang = jnp.concatenate([ang, ang, ang, ang], axis=-1)
    sign = jnp.where((jnp.arange(LANES) % QK_DIM) < QK_DIM // 2, -1.0, 1.0).astype(F32)
    return jnp.cos(ang), jnp.sin(ang) * sign[None, :]


def _filter_features(seq):
    t = jnp.linspace(0.0, 1.0, seq, dtype=F32)[:, None]
    bands = (EMB_DIM - 1) // 2
    wpos = 2.0 * math.pi * jnp.arange(seq, dtype=F32) / seq
    fb = jnp.linspace(1e-4, bands - 1, bands, dtype=F32)
    ph = wpos[:, None] * fb[None, :]
    z = jnp.concatenate([t, jnp.cos(ph), -jnp.sin(ph)], axis=-1)
    z2 = jnp.concatenate([z, jnp.roll(z[::-1], 1, axis=0)], axis=0)
    z_pad = jnp.pad(z2, ((0, 0), (0, LANES - EMB_DIM)))
    max_decay = math.log(DECAY_TARGET) / FAST_DECAY_PCT
    min_decay = math.log(DECAY_TARGET) / SLOW_DECAY_PCT
    deltas = jnp.linspace(min_decay, max_decay, D_HYENA, dtype=F32)
    return z_pad, (-jnp.abs(deltas))[None, :]


def kernel(x, norm_g, w_in, conv_w, conv_b, filt_w1, filt_b1, filt_w2, filt_b2, filt_w3, filt_b3,
           filt_freq, filt_w4, filt_bias, lam_q1, lam_k1, lam_q2, lam_k2, subln_g, w_out, final_g):
    b, seq, d = x.shape
    depth = w_in.shape[0]
    m = b * seq
    tabs = {k: jnp.asarray(v).astype(BF16) for k, v in _dft_tables(seq).items()}
    cos_t, sin_t = _rope_tables(seq)
    z_pad, neg_abs_delta = _filter_features(seq)

    kern, asum = filter_taps(z_pad, filt_w1, filt_b1, filt_w2, filt_b2, filt_w3, filt_b3, filt_freq, filt_w4,
                             neg_abs_delta)
    spec = filter_spectrum(kern, asum, tabs)

    h = x.reshape(m, d)
    for layer in range(depth):
        lam_init = 0.8 - 0.6 * math.exp(-0.3 * layer)
        hy, q0, q1, kr, vt, ag = in_proj(h, norm_g[layer], w_in[layer].astype(BF16), cos_t, sin_t, b, seq)
        hy3 = hy.reshape(b, seq, HY_COLS)
        sh = (b, seq, D_ATTN)
        y_a = attention(q0.reshape(sh), q1.reshape(sh), kr.reshape(sh), vt, ag.reshape(sh),
                        lam_q1[layer], lam_k1[layer], lam_q2[layer], lam_k2[layer], subln_g[layer], lam_init)

        csc = short_conv(hy3, conv_w[layer], conv_b[layer])
        y_h = hyena_branch(csc, hy3, spec, layer, filt_bias[layer], tabs, seq)

        h = out_proj(y_h.reshape(m, D_HYENA), y_a.reshape(m, D_ATTN), w_out[layer].astype(BF16), h,
                     final_g, final=(layer == depth - 1))
    return h.reshape(b, seq, d)
```

```python
import functools
import math

import numpy as np
import jax
import jax.numpy as jnp
from jax import lax
from jax.experimental import pallas as pl
from jax.experimental.pallas import tpu as pltpu

F32 = jnp.float32
BF16 = jnp.bfloat16

D_MODEL = 1024
D_HYENA = 512
D_ATTN = 512
N_HEADS = 4
V_DIM = 128
QK_DIM = 64
HY_PROJ = 3 * D_HYENA
D_IN = 4096
EMB_DIM = 33
FILT_ORDER = 64
DECAY_TARGET = 1e-2
FAST_DECAY_PCT = 0.3
SLOW_DECAY_PCT = 1.5
ROPE_THETA = 10000.0
NORM_EPS = 1e-6
SUBLN_EPS = 1e-5

LANES = 128
VMEM_LIMIT = 48 * 1024 * 1024


def _params(*sem):
    return pltpu.CompilerParams(dimension_semantics=sem, vmem_limit_bytes=VMEM_LIMIT)


VT_ROWS = V_DIM + 16
LOG2E = 1.4426950408889634
HY_COLS = HY_PROJ + D_HYENA


def _rope(t, cos, sin, first):
    rot = jnp.where(first, pltpu.roll(t, LANES - QK_DIM // 2, axis=1), pltpu.roll(t, QK_DIM // 2, axis=1))
    return t * cos + rot * sin


def _in_proj_kernel(x_ref, g_ref, w_ref, cos_ref, sin_ref, hy_ref, q0_ref, q1_ref, ko_ref, vt_ref, ag_ref,
                    u_ref, qk_ref):
    j = pl.program_id(1)
    w = N_HEADS * LANES

    def project():
        return jnp.dot(u_ref[...], w_ref[...], preferred_element_type=F32)

    @pl.when(j == 0)
    def _():
        x = x_ref[...]
        ms = jnp.mean(x * x, axis=-1, keepdims=True)
        u_ref[...] = (x * lax.rsqrt(ms + NORM_EPS) * g_ref[...]).astype(BF16)
        hy_ref[...] = project()

    @pl.when(j == 1)
    def _():
        hy_ref[...] = project()

    @pl.when(j == 2)
    def _():
        qk_ref[...] = project()

    @pl.when(j == 3)
    def _():
        acc = project()
        cos = cos_ref[...]
        sin = sin_ref[...]
        lane = lax.broadcasted_iota(jnp.int32, cos.shape, 1)
        first = (lane & (QK_DIM - 1)) < QK_DIM // 2
        map0 = lane < QK_DIM
        for h in range(N_HEADS):
            sl = slice(h * LANES, (h + 1) * LANES)
            r = _rope(qk_ref[:, sl], cos, sin, first) * (QK_DIM ** -0.5 * LOG2E)
            q0_ref[:, sl] = jnp.where(map0, r, 0.0).astype(BF16)
            q1_ref[:, sl] = jnp.where(map0, 0.0, r).astype(BF16)
            ko_ref[:, sl] = _rope(qk_ref[:, w + h * LANES:w + (h + 1) * LANES], cos, sin, first).astype(BF16)
        tk = vt_ref.shape[3]
        for h in range(N_HEADS):
            for c in range(vt_ref.shape[1]):
                vt_ref[h, c, :V_DIM, :] = acc[c * tk:(c + 1) * tk, h * LANES:(h + 1) * LANES].T.astype(BF16)
                vt_ref[h, c, V_DIM:, :] = jnp.ones((VT_ROWS - V_DIM, tk), BF16)
        ag_ref[...] = acc[:, w:]


def in_proj(x2, g, w_bf16, cos_t, sin_t, batch, seq, tm=1024, tk=256):
    m, d = x2.shape
    tn = 2 * N_HEADS * LANES
    assert w_bf16.shape[1] == 4 * tn and HY_COLS == 2 * tn
    w = N_HEADS * LANES
    spb = seq // tm
    half = jax.ShapeDtypeStruct((m, w), BF16)
    hspec = pl.BlockSpec((tm, w), lambda i, j: (i, 0))
    return pl.pallas_call(
        _in_proj_kernel,
        grid=(m // tm, 4),
        in_specs=[pl.BlockSpec((tm, d), lambda i, j: (i, 0)),
                  pl.BlockSpec((1, d), lambda i, j: (0, 0)),
                  pl.BlockSpec((d, tn), lambda i, j: (0, j)),
                  pl.BlockSpec((tm, LANES), lambda i, j: (i % spb, 0)),
                  pl.BlockSpec((tm, LANES), lambda i, j: (i % spb, 0))],
        out_specs=[pl.BlockSpec((tm, tn), lambda i, j: (i, jnp.minimum(j, 1))),
                   hspec, hspec, hspec,
                   pl.BlockSpec((None, N_HEADS, tm // tk, VT_ROWS, tk), lambda i, j: (i // spb, 0, i % spb, 0, 0)),
                   hspec],
        out_shape=[jax.ShapeDtypeStruct((m, HY_COLS), F32), half, half, half,
                   jax.ShapeDtypeStruct((batch, N_HEADS, seq // tk, VT_ROWS, tk), BF16),
                   jax.ShapeDtypeStruct((m, w), F32)],
        scratch_shapes=[pltpu.VMEM((tm, d), BF16), pltpu.VMEM((tm, tn), F32)],
        compiler_params=_params("parallel", "arbitrary"),
        name="in_proj",
    )(x2, g.reshape(1, d), w_bf16, cos_t, sin_t)


def _attn_kernel(q0_ref, q1_ref, k_ref, vt_ref, gate_ref, lq1_ref, lk1_ref, lq2_ref, lk2_ref,
                 subg_ref, o_ref, *, lam_init):
    tq = q0_ref.shape[0]
    nk, _, tk = vt_ref.shape
    q0 = q0_ref[...]
    q1 = q1_ref[...]

    def scores(q, kj):
        return lax.dot_general(kj, q, (((1,), (1,)), ((), ())), preferred_element_type=F32)

    def local(st):
        mx = jnp.max(st, axis=0, keepdims=True)
        return mx, jnp.exp2((st - mx).astype(BF16))

    def merge(m, acc, mx, pv):
        m_new = jnp.maximum(m, mx)
        return m_new, jnp.exp2(m - m_new) * acc + jnp.exp2(mx - m_new) * pv

    neg = jnp.full((1, tq), -1e30, F32)
    zacc = jnp.zeros((vt_ref.shape[1], tq), F32)
    m0, a0, m1, a1 = neg, zacc, neg, zacc
    k0 = k_ref[0:tk, :]
    nxt = (scores(q0, k0), scores(q1, k0))
    for j in range(nk):
        st0, st1 = nxt
        if j + 1 < nk:
            kn = k_ref[(j + 1) * tk:(j + 2) * tk, :]
            nxt = (scores(q0, kn), scores(q1, kn))
        vtj = vt_ref[j]
        mx0, p0 = local(st0)
        mx1, p1 = local(st1)
        m0, a0 = merge(m0, a0, mx0, jnp.dot(vtj, p0, preferred_element_type=F32))
        m1, a1 = merge(m1, a1, mx1, jnp.dot(vtj, p1, preferred_element_type=F32))
    l0, l1 = a0[V_DIM:V_DIM + 1], a1[V_DIM:V_DIM + 1]
    a0, a1 = a0[:V_DIM], a1[:V_DIM]

    lam = (jnp.exp(jnp.sum(lq1_ref[...] * lk1_ref[...], axis=-1, keepdims=True))
           - jnp.exp(jnp.sum(lq2_ref[...] * lk2_ref[...], axis=-1, keepdims=True)) + lam_init)
    ot = a0 / l0 - lam * (a1 / l1)
    ot = ot * lax.rsqrt(jnp.mean(ot * ot, axis=0, keepdims=True) + SUBLN_EPS)
    o = ot.T * subg_ref[...] * (1.0 - lam_init)
    g = gate_ref[...]
    o_ref[...] = (o * (g / (1.0 + jnp.exp(-g)))).astype(o_ref.dtype)


def attention(q0, q1, k, vt, gate, lq1, lk1, lq2, lk2, subg, lam_init, tq=1024):
    b, s, _ = q0.shape
    nk, tk = vt.shape[2], vt.shape[4]
    qspec = pl.BlockSpec((None, tq, LANES), lambda bi, h, qi: (bi, qi, h))
    small = pl.BlockSpec((1, QK_DIM), lambda bi, h, qi: (0, 0))
    return pl.pallas_call(
        functools.partial(_attn_kernel, lam_init=lam_init),
        grid=(b, N_HEADS, s // tq),
        in_specs=[qspec, qspec,
                  pl.BlockSpec((None, s, LANES), lambda bi, h, qi: (bi, 0, h)),
                  pl.BlockSpec((None, None, nk, VT_ROWS, tk), lambda bi, h, qi: (bi, h, 0, 0, 0)),
                  qspec,
                  small, small, small, small,
                  pl.BlockSpec((1, V_DIM), lambda bi, h, qi: (0, 0))],
        out_specs=pl.BlockSpec((None, tq, LANES), lambda bi, h, qi: (bi, qi, h)),
        out_shape=jax.ShapeDtypeStruct((b, s, D_ATTN), BF16),
        compiler_params=_params("parallel", "parallel", "arbitrary"),
        name="diff_attn",
    )(q0, q1, k, vt, gate, lq1.reshape(1, -1), lk1.reshape(1, -1), lq2.reshape(1, -1),
      lk2.reshape(1, -1), subg.reshape(1, -1))


def _short_conv_kernel(u_ref, w_ref, b_ref, o_ref):
    u = u_ref[...]
    n = u.shape[0]
    row = lax.broadcasted_iota(jnp.int32, u.shape, 0)
    prev = jnp.where(row == 0, 0.0, pltpu.roll(u, 1, axis=0))
    nxt = jnp.where(row == n - 1, 0.0, pltpu.roll(u, n - 1, axis=0))
    o_ref[...] = prev * w_ref[0:1, :] + u * w_ref[1:2, :] + nxt * w_ref[2:3, :] + b_ref[...]


def short_conv(hy3, conv_w, conv_b):
    b, s, _ = hy3.shape
    nblk = HY_PROJ // LANES
    return pl.pallas_call(
        _short_conv_kernel,
        grid=(b, nblk),
        in_specs=[pl.BlockSpec((None, s, LANES), lambda bi, j: (bi, 0, j)),
                  pl.BlockSpec((3, LANES), lambda bi, j: (0, j)),
                  pl.BlockSpec((1, LANES), lambda bi, j: (0, j))],
        out_specs=pl.BlockSpec((None, s, LANES), lambda bi, j: (bi, 0, j)),
        out_shape=jax.ShapeDtypeStruct((b, s, HY_PROJ), F32),
        compiler_params=_params("parallel", "parallel"),
        name="short_conv",
    )(hy3, conv_w, conv_b.reshape(1, -1))


def _filter_kernel(z_ref, w1_ref, b1_ref, w2_ref, b2_ref, w3_ref, b3_ref, fr_ref, w4_ref, nd_ref,
                   k_ref, asum_ref, *, half_steps):
    i = pl.program_id(0)
    hi = lax.Precision.HIGHEST
    z = z_ref[...]
    fr = fr_ref[...]
    h = jnp.sin(fr * (jnp.dot(z, w1_ref[...], precision=hi, preferred_element_type=F32) + b1_ref[...]))
    h = jnp.sin(fr * (jnp.dot(h, w2_ref[...], precision=hi, preferred_element_type=F32) + b2_ref[...]))
    h = jnp.sin(fr * (jnp.dot(h, w3_ref[...], precision=hi, preferred_element_type=F32) + b3_ref[...]))
    hf = jnp.dot(h.astype(BF16), w4_ref[...].astype(BF16), preferred_element_type=F32)
    decay = jnp.exp(z[:, 0:1] * nd_ref[...])
    tt = z.shape[0]
    row = lax.broadcasted_iota(jnp.int32, (tt, D_HYENA), 0)
    centre = jnp.logical_and(row == 0, i == half_steps)

    @pl.when(i == 0)
    def _():
        asum_ref[...] = jnp.zeros_like(asum_ref)

    for layer in range(2):
        for o in range(2):
            sl = slice(o * D_HYENA, (o + 1) * D_HYENA)
            src = slice((2 * layer + o) * D_HYENA, (2 * layer + o + 1) * D_HYENA)
            kg = jnp.where(centre, 0.0, hf[:, src] * decay)
            k_ref[layer, :, sl] = kg
            asum_ref[layer, :, sl] += jnp.sum(jnp.abs(kg), axis=0, keepdims=True)


def filter_taps(z2_pad, w1, b1, w2, b2, w3, b3, fr, w4, neg_abs_delta, tt=512):
    assert w1.shape[0] == 2, "the filter MLP packs exactly two layers into the 128 lanes"
    n = z2_pad.shape[0]
    r = FILT_ORDER
    c2 = 2 * D_HYENA
    steps = n // tt
    zero = jnp.zeros((r, r), F32)
    blockdiag = lambda w: jnp.concatenate([jnp.concatenate([w[0], zero], axis=1),
                                           jnp.concatenate([zero, w[1]], axis=1)], axis=0)
    cat = lambda v: jnp.concatenate([v[0], v[1]], axis=-1).reshape(1, 2 * r)
    w1c = jnp.pad(jnp.concatenate([w1[0], w1[1]], axis=1), ((0, LANES - EMB_DIM), (0, 0)))
    zc = jnp.zeros((r, c2), F32)
    w4c = jnp.stack([jnp.concatenate([jnp.concatenate([w4[0][:, hf * c2:(hf + 1) * c2], zc], axis=1),
                                      jnp.concatenate([zc, w4[1][:, hf * c2:(hf + 1) * c2]], axis=1)], axis=0)
                     for hf in range(2)])
    full = lambda shp: pl.BlockSpec(shp, lambda i: (0, 0))
    return pl.pallas_call(
        functools.partial(_filter_kernel, half_steps=steps // 2),
        grid=(steps,),
        in_specs=[pl.BlockSpec((tt, LANES), lambda i: (i, 0)),
                  full((LANES, 2 * r)), full((1, 2 * r)), full((2 * r, 2 * r)), full((1, 2 * r)),
                  full((2 * r, 2 * r)), full((1, 2 * r)), full((1, 2 * r)),
                  pl.BlockSpec((None, 2 * r, 2 * c2), lambda i: (i // (steps // 2), 0, 0)),
                  full((1, D_HYENA))],
        out_specs=[pl.BlockSpec((2, tt, c2), lambda i: (0, i, 0)),
                   pl.BlockSpec((2, 1, c2), lambda i: (0, 0, 0))],
        out_shape=[jax.ShapeDtypeStruct((2, n, c2), F32),
                   jax.ShapeDtypeStruct((2, 1, c2), F32)],
        compiler_params=_params("arbitrary"),
        name="hyena_filter",
    )(z2_pad, w1c, cat(b1), blockdiag(w2), cat(b2), blockdiag(w3), cat(b3), cat(fr), w4c, neg_abs_delta)


def _stack_c(m):
    return np.block([[m.real, -m.imag], [m.imag, m.real]])


def _interleave(a, axis):
    h = a.shape[axis] // 2
    idx = np.arange(2 * h).reshape(2, h).T.reshape(-1)
    return np.take(a, idx, axis=axis)


@functools.lru_cache(maxsize=None)
def _dft_tables(seq):
    n = 2 * seq
    r1 = LANES
    r2h = seq // r1
    n1 = np.arange(r1)
    k2 = np.arange(r1)
    n2 = np.arange(r2h)
    e = (k2[None, :, None] * (n1[:, None, None] + r1 * n2[None, None, :])) % n
    t = np.exp(-2j * np.pi * e / n)
    g_fwd = np.concatenate([np.concatenate([t.real, -t.imag], axis=2),
                            np.concatenate([t.imag, t.real], axis=2)], axis=1)
    g_fwd = _interleave(g_fwd, 1)
    n2f = np.arange(2 * r2h)
    e = (k2[None, :, None] * (n1[:, None, None] + r1 * n2f[None, None, :])) % n
    tf = np.exp(-2j * np.pi * e / n)
    g_real = _interleave(np.concatenate([tf.real, tf.imag], axis=1), 1)
    f = np.exp(-2j * np.pi * ((n1[:, None] * n1[None, :]) % r1) / r1)
    m3 = _interleave(_stack_c(f), 1)
    e = (n1[None, :, None] * (r1 * n1[None, None, :] + k2[:, None, None])) % n
    einv = np.exp(2j * np.pi * e / n)
    g_inv = np.concatenate([np.concatenate([einv.real, -einv.imag], axis=2),
                            np.concatenate([einv.imag, einv.real], axis=2)], axis=1)
    g_inv = _interleave(g_inv, 1)
    d = np.exp(2j * np.pi * ((n2[:, None] * k2[None, :]) % r1) / r1)
    m5 = _interleave(_stack_c(d), 1)
    tabs = dict(g_fwd=g_fwd, g_real=g_real, m3=m3, g_inv=g_inv, m5=m5)
    return {k: v.astype(np.float32) for k, v in tabs.items()}


def _pack(y):
    return pltpu.bitcast(y.astype(BF16), jnp.uint32)


def _unpack(w):
    return pltpu.bitcast(w, BF16)


GRP = 8


def _filt_stage1_kernel(x_ref, s_ref, g_ref, o_ref):
    scale = 1.0 / s_ref[...]
    half = x_ref.shape[0] // 2
    for j in range(GRP):
        x = jnp.concatenate([x_ref[:half, j, :], x_ref[half:, j, :]], axis=0)
        y = jnp.dot(g_ref[j], (x * scale).astype(BF16), preferred_element_type=F32)
        o_ref[j] = _pack(y)


def _filt_stage2_kernel(b_ref, m_ref, o_ref, *, inv_n):
    r1 = o_ref.shape[2]
    for j in range(GRP):
        x = jnp.dot(m_ref[...], _unpack(b_ref[:, j, :]), preferred_element_type=F32) * inv_n
        o_ref[0, j] = x[:r1].astype(o_ref.dtype)
        o_ref[1, j] = x[r1:].astype(o_ref.dtype)


def filter_spectrum(kern, asum, tabs):
    c = D_HYENA
    r1 = LANES
    nl = kern.shape[0]
    r2 = kern.shape[1] // r1
    x4 = kern.reshape(nl, r2, r1, 2 * c)
    s1 = pl.pallas_call(
        _filt_stage1_kernel,
        grid=(nl, 2, r1 // GRP),
        in_specs=[pl.BlockSpec((None, r2, GRP, c), lambda l, o, i: (l, 0, i, o)),
                  pl.BlockSpec((None, 1, c), lambda l, o, i: (l, 0, o)),
                  pl.BlockSpec((GRP, 2 * r1, r2), lambda l, o, i: (i, 0, 0))],
        out_specs=pl.BlockSpec((None, None, GRP, r1, c), lambda l, o, i: (l, o, i, 0, 0)),
        out_shape=jax.ShapeDtypeStruct((nl, 2, r1, r1, c), jnp.uint32),
        compiler_params=_params("parallel", "parallel", "parallel"),
        name="filt_dft1",
    )(x4, asum, tabs["g_real"])
    return pl.pallas_call(
        functools.partial(_filt_stage2_kernel, inv_n=1.0 / kern.shape[1]),
        grid=(nl, 2, r1 // GRP),
        in_specs=[pl.BlockSpec((None, None, r1, GRP, c), lambda l, o, i: (l, o, 0, i, 0)),
                  pl.BlockSpec((2 * r1, 2 * r1), lambda l, o, i: (0, 0))],
        out_specs=pl.BlockSpec((None, None, 2, GRP, r1, c), lambda l, o, i: (l, o, 0, i, 0, 0)),
        out_shape=jax.ShapeDtypeStruct((nl, 2, 2, r1, r1, c), BF16),
        compiler_params=_params("parallel", "parallel", "parallel"),
        name="filt_dft2",
    )(s1, tabs["m3"])


def _conv_stage1_kernel(x_ref, g_ref, o_ref):
    for j in range(GRP):
        zin = jnp.concatenate([x_ref[0, :, j, :], x_ref[1, :, j, :]], axis=0).astype(BF16)
        o_ref[j] = _pack(jnp.dot(g_ref[j], zin, preferred_element_type=F32))


def _conv_stage2_kernel(b_ref, kf_ref, m3_ref, ginv_ref, o_ref):
    r1 = o_ref.shape[1]
    for j in range(GRP):
        x = jnp.dot(m3_ref[...], _unpack(b_ref[:, j, :]), preferred_element_type=F32)
        xr, xi = x[:r1], x[r1:]
        kr, ki = kf_ref[0, j].astype(F32), kf_ref[1, j].astype(F32)
        y = jnp.concatenate([xr * kr - xi * ki, xr * ki + xi * kr], axis=0).astype(BF16)
        o_ref[j] = _pack(jnp.dot(ginv_ref[j], y, preferred_element_type=F32))


def _conv_stage3_mid_kernel(q_ref, m5_ref, u_ref, x1_ref, d_ref, g_ref, o_ref, s1_ref):
    half = u_ref.shape[1]
    d = d_ref[...]
    for j in range(GRP):
        y = jnp.dot(m5_ref[...], _unpack(q_ref[:, j, :]), preferred_element_type=F32)
        zs = []
        for bi in range(2):
            conv = y[bi * half:(bi + 1) * half] + u_ref[bi, :, j, :] * d
            z = x1_ref[bi, :, j, :] * conv
            o_ref[bi, :, j, :] = z
            zs.append(z)
        zin = jnp.concatenate(zs, axis=0).astype(BF16)
        s1_ref[j] = _pack(jnp.dot(g_ref[j], zin, preferred_element_type=F32))


def _conv_stage3_last_kernel(q_ref, m5_ref, u_ref, x2_ref, gate_ref, d_ref, o_ref):
    half = u_ref.shape[1]
    d = d_ref[...]
    for j in range(GRP):
        y = jnp.dot(m5_ref[...], _unpack(q_ref[:, j, :]), preferred_element_type=F32)
        for bi in range(2):
            g = gate_ref[bi, :, j, :]
            conv = y[bi * half:(bi + 1) * half] + u_ref[bi, :, j, :] * d
            o_ref[bi, :, j, :] = x2_ref[bi, :, j, :] * conv * (g / (1.0 + jnp.exp(-g)))


def _conv_dft1(src4, blk, tabs):
    c = D_HYENA
    r1 = LANES
    b, r2h = src4.shape[0], src4.shape[1]
    return pl.pallas_call(
        _conv_stage1_kernel,
        grid=(r1 // GRP,),
        in_specs=[pl.BlockSpec((b, r2h, GRP, c), lambda i: (0, 0, i, blk)),
                  pl.BlockSpec((GRP, 2 * r1, 2 * r2h), lambda i: (i, 0, 0))],
        out_specs=pl.BlockSpec((GRP, r1, c), lambda i: (i, 0, 0)),
        out_shape=jax.ShapeDtypeStruct((r1, r1, c), jnp.uint32),
        compiler_params=_params("parallel"),
        name="conv_dft1",
    )(src4, tabs["g_fwd"])


def _conv_dft2(s1, spec, layer, order, tabs):
    c = D_HYENA
    r1 = LANES
    return pl.pallas_call(
        _conv_stage2_kernel,
        grid=(r1 // GRP,),
        in_specs=[pl.BlockSpec((r1, GRP, c), lambda i: (0, i, 0)),
                  pl.BlockSpec((None, None, 2, GRP, r1, c), lambda i: (layer, order, 0, i, 0, 0)),
                  pl.BlockSpec((2 * r1, 2 * r1), lambda i: (0, 0)),
                  pl.BlockSpec((GRP, 2 * r1, 2 * r1), lambda i: (i, 0, 0))],
        out_specs=pl.BlockSpec((GRP, r1, c), lambda i: (i, 0, 0)),
        out_shape=jax.ShapeDtypeStruct((r1, r1, c), jnp.uint32),
        compiler_params=_params("parallel"),
        name="conv_dft2",
    )(s1, spec, tabs["m3"], tabs["g_inv"])


def hyena_branch(csc, hy3, spec, layer, filt_bias, tabs, seq):
    c = D_HYENA
    r1 = LANES
    r2h = seq // r1
    b = csc.shape[0]
    steps = r1 // GRP
    csc4 = csc.reshape(b, r2h, r1, 3 * c)
    hy4 = hy3.reshape(b, r2h, r1, HY_COLS)
    qspec = pl.BlockSpec((r1, GRP, c), lambda i: (0, i, 0))
    m5spec = pl.BlockSpec((2 * r2h, 2 * r1), lambda i: (0, 0))
    dspec = pl.BlockSpec((1, c), lambda i: (0, 0))
    tblk = lambda blk: pl.BlockSpec((b, r2h, GRP, c), lambda i: (0, 0, i, blk))

    q1 = _conv_dft2(_conv_dft1(csc4, 0, tabs), spec, layer, 0, tabs)
    z, s1 = pl.pallas_call(
        _conv_stage3_mid_kernel,
        grid=(steps,),
        in_specs=[qspec, m5spec, tblk(0), tblk(1), dspec,
                  pl.BlockSpec((GRP, 2 * r1, 2 * r2h), lambda i: (i, 0, 0))],
        out_specs=[tblk(0), pl.BlockSpec((GRP, r1, c), lambda i: (i, 0, 0))],
        out_shape=[jax.ShapeDtypeStruct((b, r2h, r1, c), F32),
                   jax.ShapeDtypeStruct((r1, r1, c), jnp.uint32)],
        compiler_params=_params("parallel"),
        name="conv_idft_mid",
    )(q1, tabs["m5"], csc4, csc4, filt_bias[0:1], tabs["g_fwd"])

    q2 = _conv_dft2(s1, spec, layer, 1, tabs)
    y = pl.pallas_call(
        _conv_stage3_last_kernel,
        grid=(steps,),
        in_specs=[qspec, m5spec, tblk(0), tblk(2), tblk(HY_PROJ // c), dspec],
        out_specs=tblk(0),
        out_shape=jax.ShapeDtypeStruct((b, r2h, r1, c), F32),
        compiler_params=_params("parallel"),
        name="conv_idft_last",
    )(q2, tabs["m5"], z, csc4, hy4, filt_bias[1:2])
    return y.reshape(b, seq, c)


def _out_proj_kernel(yh_ref, ya_ref, w_ref, h_ref, g_ref, o_ref, *, final):
    k = yh_ref.shape[1]
    acc = jnp.dot(yh_ref[...].astype(BF16), w_ref[:k, :], preferred_element_type=F32)
    acc = acc + jnp.dot(ya_ref[...].astype(BF16), w_ref[k:, :], preferred_element_type=F32)
    hn = h_ref[...] + acc
    if final:
        ms = jnp.mean(hn * hn, axis=-1, keepdims=True)
        hn = hn * lax.rsqrt(ms + NORM_EPS) * g_ref[...]
    o_ref[...] = hn


def out_proj(yh, ya, w_bf16, h2, g, final, tm=512):
    m, d = h2.shape
    k = yh.shape[1]
    return pl.pallas_call(
        functools.partial(_out_proj_kernel, final=final),
        grid=(m // tm,),
        in_specs=[pl.BlockSpec((tm, k), lambda i: (i, 0)),
                  pl.BlockSpec((tm, k), lambda i: (i, 0)),
                  pl.BlockSpec((2 * k, d), lambda i: (0, 0)),
                  pl.BlockSpec((tm, d), lambda i: (i, 0)),
                  pl.BlockSpec((1, d), lambda i: (0, 0))],
        out_specs=pl.BlockSpec((tm, d), lambda i: (i, 0)),
        out_shape=jax.ShapeDtypeStruct((m, d), F32),
        compiler_params=_params("parallel"),
        name="out_proj",
    )(yh, ya, w_bf16, h2, g.reshape(1, d))


def _rope_tables(seq):
    inv_freq = 1.0 / (ROPE_THETA ** (jnp.arange(0, QK_DIM, 2, dtype=F32) / QK_DIM))
    ang = jnp.arange(seq, dtype=F32)[:, None] * inv_freq[None, :]
    ang = jnp.concatenate([ang, ang, ang, ang], axis=-1)
    sign = jnp.where((jnp.arange(LANES) % QK_DIM) < QK_DIM // 2, -1.0, 1.0).astype(F32)
    return jnp.cos(ang), jnp.sin(ang) * sign[None, :]


def _filter_features(seq):
    t = jnp.linspace(0.0, 1.0, seq, dtype=F32)[:, None]
    bands = (EMB_DIM - 1) // 2
    wpos = 2.0 * math.pi * jnp.arange(seq, dtype=F32) / seq
    fb = jnp.linspace(1e-4, bands - 1, bands, dtype=F32)
    ph = wpos[:, None] * fb[None, :]
    z = jnp.concatenate([t, jnp.cos(ph), -jnp.sin(ph)], axis=-1)
    z2 = jnp.concatenate([z, jnp.roll(z[::-1], 1, axis=0)], axis=0)
    z_pad = jnp.pad(z2, ((0, 0), (0, LANES - EMB_DIM)))
    max_decay = math.log(DECAY_TARGET) / FAST_DECAY_PCT
    min_decay = math.log(DECAY_TARGET) / SLOW_DECAY_PCT
    deltas = jnp.linspace(min_decay, max_decay, D_HYENA, dtype=F32)
    return z_pad, (-jnp.abs(deltas))[None, :]


def kernel(x, norm_g, w_in, conv_w, conv_b, filt_w1, filt_b1, filt_w2, filt_b2, filt_w3, filt_b3,
           filt_freq, filt_w4, filt_bias, lam_q1, lam_k1, lam_q2, lam_k2, subln_g, w_out, final_g):
    b, seq, d = x.shape
    depth = w_in.shape[0]
    m = b * seq
    tabs = {k: jnp.asarray(v).astype(BF16) for k, v in _dft_tables(seq).items()}
    cos_t, sin_t = _rope_tables(seq)
    z_pad, neg_abs_delta = _filter_features(seq)

    kern, asum = filter_taps(z_pad, filt_w1, filt_b1, filt_w2, filt_b2, filt_w3, filt_b3, filt_freq, filt_w4,
                             neg_abs_delta)
    spec = filter_spectrum(kern, asum, tabs)

    h = x.reshape(m, d)
    for layer in range(depth):
        lam_init = 0.8 - 0.6 * math.exp(-0.3 * layer)
        hy, q0, q1, kr, vt, ag = in_proj(h, norm_g[layer], w_in[layer].astype(BF16), cos_t, sin_t, b, seq)
        hy3 = hy.reshape(b, seq, HY_COLS)
        sh = (b, seq, D_ATTN)
        y_a = attention(q0.reshape(sh), q1.reshape(sh), kr.reshape(sh), vt, ag.reshape(sh),
                        lam_q1[layer], lam_k1[layer], lam_q2[layer], lam_k2[layer], subln_g[layer], lam_init)

        csc = short_conv(hy3, conv_w[layer], conv_b[layer])
        y_h = hyena_branch(csc, hy3, spec, layer, filt_bias[layer], tabs, seq)

        h = out_proj(y_h.reshape(m, D_HYENA), y_a.reshape(m, D_ATTN), w_out[layer].astype(BF16), h,
                     final_g, final=(layer == depth - 1))
    return h.reshape(b, seq, d)
```
